```python
import jax, jax.numpy as jnp
from jax import lax
import numpy as np

D_MODEL = 1024
BATCH = 32
SEQ = 2048
DEPTH = 2
DEC_BATCH = 16
DEC_SEQ = 64
PAST_LEN = 2048

CHUNK = 64
N_MIXERS = 2
N_GLA_LAYERS = (DEPTH + 1) // 2
N_SWA_LAYERS = DEPTH // 2
GLA_HEADS = 4
GLA_DK = D_MODEL // 2 // GLA_HEADS
GLA_DV = D_MODEL // GLA_HEADS
GLA_GATE_RANK = 16
GLA_GATE_NORM = 16.0
SWA_HEAD_DIM = 64
SWA_Q_HEADS = D_MODEL // SWA_HEAD_DIM
SWA_KV_HEADS = 4
SWA_GROUP = SWA_Q_HEADS // SWA_KV_HEADS
WINDOW = 128
WINDOW_CHUNKS = WINDOW // CHUNK
D_FF = 4 * D_MODEL
EPS = 1e-6

kernel_name = "hybrid_gla_swa_sink_streaming_step"


def rms_norm(x, g):
    xf = x.astype(jnp.float32)
    y = xf * lax.rsqrt(jnp.mean(xf * xf, axis=-1, keepdims=True) + EPS)
    return (y * g.astype(jnp.float32)).astype(x.dtype)


def squared_relu_mlp(h, w_up, w_down):
    return jnp.square(jax.nn.relu(h @ w_up)) @ w_down


def gla_chunk(q, k, v, logf, s0):
    L = q.shape[1]
    qf, kf, vf = q.astype(jnp.float32), k.astype(jnp.float32), v.astype(jnp.float32)
    b = jnp.cumsum(logf, axis=1)
    inter = jnp.einsum('blhk,bhkv->blhv', qf * jnp.exp(b), s0)
    causal = jnp.tril(jnp.ones((L, L), dtype=bool))[None, :, :, None, None]
    diff = b[:, :, None] - b[:, None, :]
    decay = jnp.exp(jnp.where(causal, diff, -jnp.inf))
    attn = jnp.einsum('blhk,bmhk,blmhk->bhlm', qf, kf, decay)
    intra = jnp.einsum('bhlm,bmhv->blhv', attn, vf)
    b_last = b[:, -1]
    k_dec = kf * jnp.exp(b_last[:, None] - b)
    s_new = jnp.exp(b_last)[..., None] * s0 + jnp.einsum('blhk,blhv->bhkv', k_dec, vf)
    return inter + intra, s_new


def gla_mixer(h, s0, w_in, w_gate_a, w_gate_b, b_gate, g_onorm, w_out):
    B, T, _ = h.shape
    dqk = GLA_HEADS * GLA_DK
    dv = GLA_HEADS * GLA_DV
    proj = h @ w_in
    q = proj[..., :dqk].reshape(B, T, GLA_HEADS, GLA_DK) * (GLA_DK ** -0.5)
    k = proj[..., dqk:2 * dqk].reshape(B, T, GLA_HEADS, GLA_DK)
    v = proj[..., 2 * dqk:2 * dqk + dv].reshape(B, T, GLA_HEADS, GLA_DV)
    r = proj[..., 2 * dqk + dv:]
    gate_pre = ((h @ w_gate_a) @ w_gate_b + b_gate).astype(jnp.float32)
    logf = (jax.nn.log_sigmoid(gate_pre) / GLA_GATE_NORM).reshape(B, T, GLA_HEADS, GLA_DK)
    lc = min(T, CHUNK)
    n = T // lc

    def to_chunks(a):
        return jnp.moveaxis(a.reshape((B, n, lc) + a.shape[2:]), 1, 0)

    def step(s, xs):
        qc, kc, vc, fc = xs
        o, s = gla_chunk(qc, kc, vc, fc, s)
        return s, o

    s_fin, o = lax.scan(step, s0.astype(jnp.float32),
                        (to_chunks(q), to_chunks(k), to_chunks(v), to_chunks(logf)))
    o = jnp.moveaxis(o, 0, 1).reshape(B, T, GLA_HEADS, GLA_DV)
    o = o * lax.rsqrt(jnp.mean(o * o, axis=-1, keepdims=True) + EPS)
    o = o.reshape(B, T, dv) * g_onorm.astype(jnp.float32)
    o = (o * jax.nn.silu(r.astype(jnp.float32))).astype(h.dtype)
    return o @ w_out, s_fin.astype(h.dtype)


def swa_qkv(h, w_in, b_in):
    B, T, _ = h.shape
    dq = SWA_Q_HEADS * SWA_HEAD_DIM
    dkv = SWA_KV_HEADS * SWA_HEAD_DIM
    proj = h @ w_in + b_in
    q = proj[..., :dq].reshape(B, T, SWA_KV_HEADS, SWA_GROUP, SWA_HEAD_DIM)
    k = proj[..., dq:dq + dkv].reshape(B, T, SWA_KV_HEADS, SWA_HEAD_DIM)
    v = proj[..., dq + dkv:].reshape(B, T, SWA_KV_HEADS, SWA_HEAD_DIM)
    return q, k, v


def sink_attention(q, k, v, mask, sinks):
    s = jnp.einsum('bnqkgd,bnskd->bnkgqs', q, k).astype(jnp.float32) * (SWA_HEAD_DIM ** -0.5)
    s = jnp.where(mask, s, -jnp.inf)
    sink = sinks.astype(jnp.float32).reshape(SWA_KV_HEADS, SWA_GROUP)[:, :, None, None]
    m = jnp.maximum(jnp.max(s, axis=-1, keepdims=True), sink)
    p = jnp.exp(s - m)
    p = p / (jnp.sum(p, axis=-1, keepdims=True) + jnp.exp(sink - m))
    return jnp.einsum('bnkgqs,bnskd->bnqkgd', p.astype(v.dtype), v)


def swa_prompt(h, w_in, b_in, sinks, w_out, b_out):
    B, T, _ = h.shape
    n = T // CHUNK
    q, k, v = swa_qkv(h, w_in, b_in)
    pad = WINDOW_CHUNKS * CHUNK

    def band(a):
        ap = jnp.pad(a, ((0, 0), (pad, 0), (0, 0), (0, 0)))
        ap = ap.reshape(B, n + WINDOW_CHUNKS, CHUNK, SWA_KV_HEADS, SWA_HEAD_DIM)
        return jnp.concatenate([ap[:, j:j + n] for j in range(WINDOW_CHUNKS + 1)], axis=2)

    kb, vb = band(k), band(v)
    key_chunk = jnp.arange(n)[:, None] - WINDOW_CHUNKS + jnp.arange((WINDOW_CHUNKS + 1) * CHUNK)[None, :] // CHUNK
    mask = (key_chunk >= 0)[None, :, None, None, None, :]
    qb = q.reshape(B, n, CHUNK, SWA_KV_HEADS, SWA_GROUP, SWA_HEAD_DIM)
    o = sink_attention(qb, kb, vb, mask, sinks)
    o = o.reshape(B, T, SWA_Q_HEADS * SWA_HEAD_DIM) @ w_out + b_out
    keep = min(WINDOW, T)
    return o, k[:, T - keep:], v[:, T - keep:]


def swa_sample(h, cache_k, cache_v, w_in, b_in, sinks, w_out, b_out):
    B, T, _ = h.shape
    q, k, v = swa_qkv(h, w_in, b_in)
    kf = jnp.concatenate([cache_k.astype(k.dtype), k], axis=1)
    vf = jnp.concatenate([cache_v.astype(v.dtype), v], axis=1)
    o = sink_attention(q[:, None], kf[:, None], vf[:, None], True, sinks)
    o = o.reshape(B, T, SWA_Q_HEADS * SWA_HEAD_DIM) @ w_out + b_out
    keep = cache_k.shape[1]
    return o, kf[:, kf.shape[1] - keep:], vf[:, vf.shape[1] - keep:]


def trunk(x, gla_s0, swa_k0, swa_v0, gla_w_in, gla_w_gate_a, gla_w_gate_b, gla_b_gate, gla_g_onorm,
          gla_w_out, swa_w_in, swa_b_in, swa_sinks, swa_w_out, swa_b_out, norm_g, mlp_w_up, mlp_w_down):
    new_s, new_k, new_v = [], [], []
    for i in range(DEPTH):
        j = i // N_MIXERS
        h = rms_norm(x, norm_g[i, 0])
        if i % N_MIXERS == 0:
            o, s = gla_mixer(h, gla_s0[j], gla_w_in[j], gla_w_gate_a[j], gla_w_gate_b[j], gla_b_gate[j],
                             gla_g_onorm[j], gla_w_out[j])
            new_s.append(s)
        else:
            if swa_k0 is None:
                o, kc, vc = swa_prompt(h, swa_w_in[j], swa_b_in[j], swa_sinks[j], swa_w_out[j], swa_b_out[j])
            else:
                o, kc, vc = swa_sample(h, swa_k0[j], swa_v0[j], swa_w_in[j], swa_b_in[j], swa_sinks[j],
                                       swa_w_out[j], swa_b_out[j])
            new_k.append(kc)
            new_v.append(vc)
        x = x + rms_norm(o, norm_g[i, 1])
        h = rms_norm(x, norm_g[i, 2])
        x = x + rms_norm(squared_relu_mlp(h, mlp_w_up[i], mlp_w_down[i]), norm_g[i, 3])
    return x, jnp.stack(new_s), jnp.stack(new_k), jnp.stack(new_v)


def setup_inputs(seed: int = 0) -> dict:
    key = jax.random.key(seed)
    ks = jax.random.split(key, 20)

    def nrm(k, shape, scale):
        return jax.random.normal(k, shape, jnp.float32) * scale

    dqk = GLA_HEADS * GLA_DK
    dv = GLA_HEADS * GLA_DV
    dq = SWA_Q_HEADS * SWA_HEAD_DIM
    swa_cols = (SWA_Q_HEADS + 2 * SWA_KV_HEADS) * SWA_HEAD_DIM
    cache_len = min(WINDOW, PAST_LEN)
    return {
        "x_prompt": nrm(ks[0], (BATCH, SEQ, D_MODEL), 1.0),
        "x_sample": nrm(ks[1], (DEC_BATCH, DEC_SEQ, D_MODEL), 1.0),
        "state_gla": nrm(ks[2], (N_GLA_LAYERS, DEC_BATCH, GLA_HEADS, GLA_DK, GLA_DV), 0.5),
        "cache_swa_k": nrm(ks[3], (N_SWA_LAYERS, DEC_BATCH, cache_len, SWA_KV_HEADS, SWA_HEAD_DIM), 1.0),
        "cache_swa_v": nrm(ks[4], (N_SWA_LAYERS, DEC_BATCH, cache_len, SWA_KV_HEADS, SWA_HEAD_DIM), 1.0),
        "gla_w_in": nrm(ks[5], (N_GLA_LAYERS, D_MODEL, 2 * dqk + 2 * dv), D_MODEL ** -0.5),
        "gla_w_gate_a": nrm(ks[6], (N_GLA_LAYERS, D_MODEL, GLA_GATE_RANK), D_MODEL ** -0.5),
        "gla_w_gate_b": nrm(ks[7], (N_GLA_LAYERS, GLA_GATE_RANK, dqk), GLA_GATE_RANK ** -0.5),
        "gla_b_gate": nrm(ks[8], (N_GLA_LAYERS, dqk), 0.1),
        "gla_g_onorm": 1.0 + nrm(ks[9], (N_GLA_LAYERS, dv), 0.1),
        "gla_w_out": nrm(ks[10], (N_GLA_LAYERS, dv, D_MODEL), dv ** -0.5),
        "swa_w_in": nrm(ks[11], (N_SWA_LAYERS, D_MODEL, swa_cols), D_MODEL ** -0.5),
        "swa_b_in": nrm(ks[12], (N_SWA_LAYERS, swa_cols), 0.02),
        "swa_sinks": nrm(ks[13], (N_SWA_LAYERS, SWA_Q_HEADS), 1.0),
        "swa_w_out": nrm(ks[14], (N_SWA_LAYERS, dq, D_MODEL), dq ** -0.5),
        "swa_b_out": nrm(ks[15], (N_SWA_LAYERS, D_MODEL), 0.02),
        "norm_g": 1.0 + nrm(ks[16], (DEPTH, 4, D_MODEL), 0.1),
        "mlp_w_up": nrm(ks[17], (DEPTH, D_MODEL, D_FF), D_MODEL ** -0.5),
        "mlp_w_down": nrm(ks[18], (DEPTH, D_FF, D_MODEL), D_FF ** -0.5),
    }


def reference(x_prompt, x_sample, state_gla, cache_swa_k, cache_swa_v, gla_w_in, gla_w_gate_a, gla_w_gate_b,
              gla_b_gate, gla_g_onorm, gla_w_out, swa_w_in, swa_b_in, swa_sinks, swa_w_out, swa_b_out,
              norm_g, mlp_w_up, mlp_w_down):
    gla_zero = jnp.zeros((N_GLA_LAYERS, x_prompt.shape[0], GLA_HEADS, GLA_DK, GLA_DV), x_prompt.dtype)
    y_prompt, state_gla_prompt, cache_swa_k_prompt, cache_swa_v_prompt = trunk(
        x_prompt, gla_zero, None, None, gla_w_in, gla_w_gate_a, gla_w_gate_b, gla_b_gate, gla_g_onorm,
        gla_w_out, swa_w_in, swa_b_in, swa_sinks, swa_w_out, swa_b_out, norm_g, mlp_w_up, mlp_w_down)
    y_sample, state_gla_sample, cache_swa_k_sample, cache_swa_v_sample = trunk(
        x_sample, state_gla, cache_swa_k, cache_swa_v, gla_w_in, gla_w_gate_a, gla_w_gate_b, gla_b_gate,
        gla_g_onorm, gla_w_out, swa_w_in, swa_b_in, swa_sinks, swa_w_out, swa_b_out, norm_g, mlp_w_up,
        mlp_w_down)
    return (y_prompt, y_sample, state_gla_prompt, state_gla_sample, cache_swa_k_prompt, cache_swa_v_prompt,
            cache_swa_k_sample, cache_swa_v_sample)
```

```python
import functools

import numpy as np
import jax
import jax.numpy as jnp
from jax import lax
from jax.experimental import pallas as pl
from jax.experimental.pallas import tpu as pltpu

F32 = jnp.float32
BF16 = jnp.bfloat16

D_MODEL = 1024
D_FF = 4 * D_MODEL
EPS = 1e-6
CHUNK = 64
GLA_HEADS = 4
GLA_DK = 128
GLA_DV = 256
GLA_DQK = GLA_HEADS * GLA_DK
GLA_DVT = GLA_HEADS * GLA_DV
GLA_GATE_RANK = 16
GLA_GATE_NORM = 16.0
GLA_GATE_PAD = 128
GLA_PROJ_COLS = 2 * GLA_DQK + 2 * GLA_DVT + GLA_GATE_PAD
GLA_MAX_CHUNK = 256
GLA_SAFE_DECAY = 60.0
SWA_HD = 64
SWA_QH = 16
SWA_KVH = 4
SWA_GROUP = 4
SWA_DQ = SWA_QH * SWA_HD
SWA_DKV = SWA_KVH * SWA_HD
SWA_COLS = SWA_DQ + 2 * SWA_DKV
WINDOW = 128
ROW_TILE = 512
VMEM_LIMIT_BYTES = 56 * 1024 * 1024


def _dot(a, b):
    return jnp.dot(a, b, preferred_element_type=F32)


def _dot_nt(a, b):
    return lax.dot_general(a, b, (((1,), (1,)), ((), ())), preferred_element_type=F32)


def _dot_tn(a, b):
    return lax.dot_general(a, b, (((0,), (0,)), ((), ())), preferred_element_type=F32)


def _rms(x, g):
    return x * lax.rsqrt(jnp.mean(x * x, axis=-1, keepdims=True) + EPS) * g


def _split_bf16(x):
    hi = x.astype(BF16)
    lo = (x - hi.astype(F32)).astype(BF16)
    return hi, lo


def _params(semantics):
    return pltpu.CompilerParams(dimension_semantics=semantics, vmem_limit_bytes=VMEM_LIMIT_BYTES)


def _norm_proj_kernel(x_ref, g_ref, w_ref, b_ref, o_ref):
    h = _rms(x_ref[...], g_ref[0:1, :]).astype(BF16)
    o_ref[...] = _dot(h, w_ref[...]) + b_ref[...]


def _norm_proj(x, g4, w, bias, tm):
    n, d = x.shape
    cols = w.shape[1]
    return pl.pallas_call(
        _norm_proj_kernel,
        grid=(n // tm,),
        in_specs=[
            pl.BlockSpec((tm, d), lambda i: (i, 0)),
            pl.BlockSpec((4, d), lambda i: (0, 0)),
            pl.BlockSpec((d, cols), lambda i: (0, 0)),
            pl.BlockSpec((1, cols), lambda i: (0, 0)),
        ],
        out_specs=pl.BlockSpec((tm, cols), lambda i: (i, 0)),
        out_shape=jax.ShapeDtypeStruct((n, cols), F32),
        compiler_params=_params(("arbitrary",)),
        name="norm_proj",
    )(x, g4, w, bias)


def _gla_level_matrices(c):
    levels = []
    s = c
    while s >= 2:
        levels.append(s)
        s //= 2
    t = np.arange(c)[None, :]
    r = np.arange(c)[:, None]
    mats = []
    for s in levels:
        mid = (r // s) * s + s // 2
        mq = (r >= mid) & (t >= mid) & (t <= r)
        mk = (r < mid) & (t > r) & (t <= mid - 1)
        mats.append(mq)
        mats.append(mk)
    return np.stack(mats).astype(np.float32), levels


def _gla_kernel(*refs, c, levels, has_s0):
    if has_s0:
        proj_ref, wb_ref, bg_ref, gon_ref, tri_ref, lev_ref, s0_ref, og_ref, s_ref, a_scr = refs
    else:
        proj_ref, wb_ref, bg_ref, gon_ref, tri_ref, lev_ref, og_ref, s_ref, a_scr = refs
        s0_ref = None
    t = pl.program_id(1)

    @pl.when(t == 0)
    def _():
        if has_s0:
            s_ref[...] = s0_ref[...]
        else:
            s_ref[...] = jnp.zeros(s_ref.shape, F32)

    ga = proj_ref[:, 2 * GLA_DQK + 2 * GLA_DVT:].astype(BF16)
    gp = _dot(ga, wb_ref[...]) + bg_ref[...]
    lf = (jnp.minimum(gp, 0.0) - jnp.log(1.0 + jnp.exp(-jnp.abs(gp)))) * (1.0 / GLA_GATE_NORM)
    lf_hi, lf_lo = _split_bf16(lf)
    tri = tri_ref[...]
    b = _dot(tri, lf_hi) + _dot(tri, lf_lo)
    b_last = b[c - 1:c, :]
    ones_c = jnp.ones((c, GLA_DK), BF16)
    scale = GLA_DK ** -0.5

    row = lax.broadcasted_iota(jnp.int32, (c, c), 0)
    col = lax.broadcasted_iota(jnp.int32, (c, c), 1)

    safe = jnp.min(b) >= -GLA_SAFE_DECAY

    @pl.when(safe)
    def _():
        for h in range(GLA_HEADS):
            cs = slice(h * GLA_DK, (h + 1) * GLA_DK)
            bh = b[:, cs]
            qe = (proj_ref[:, cs] * scale * jnp.exp(bh)).astype(BF16)
            ke = (proj_ref[:, GLA_DQK + h * GLA_DK:GLA_DQK + (h + 1) * GLA_DK] * jnp.exp(-bh)).astype(BF16)
            a_scr[h] = jnp.where(row >= col, _dot_nt(qe, ke), 0.0)

    @pl.when(jnp.logical_not(safe))
    def _():
        for h in range(GLA_HEADS):
            cs = slice(h * GLA_DK, (h + 1) * GLA_DK)
            q = proj_ref[:, cs] * scale
            k = proj_ref[:, GLA_DQK + h * GLA_DK:GLA_DQK + (h + 1) * GLA_DK]
            a_scr[h] = jnp.where(row == col, _dot_nt(q.astype(BF16), k.astype(BF16)), 0.0)
        for li, s in enumerate(levels):
            mq = lev_ref[2 * li]
            mk = lev_ref[2 * li + 1]
            dq = _dot(mq, lf_hi) + _dot(mq, lf_lo)
            dk = _dot(mk, lf_hi) + _dot(mk, lf_lo)
            half = s // 2
            sh = s.bit_length() - 1
            same = lax.shift_right_logical(row, sh) == lax.shift_right_logical(col, sh)
            mask = same & ((row & (s - 1)) >= half) & ((col & (s - 1)) < half)
            for h in range(GLA_HEADS):
                cs = slice(h * GLA_DK, (h + 1) * GLA_DK)
                qq = (proj_ref[:, cs] * scale * jnp.exp(dq[:, cs])).astype(BF16)
                kk = (proj_ref[:, GLA_DQK + h * GLA_DK:GLA_DQK + (h + 1) * GLA_DK]
                      * jnp.exp(dk[:, cs])).astype(BF16)
                a_scr[h] = a_scr[h] + jnp.where(mask, _dot_nt(qq, kk), 0.0)

    for h in range(GLA_HEADS):
        cs = slice(h * GLA_DK, (h + 1) * GLA_DK)
        vs = slice(2 * GLA_DQK + h * GLA_DV, 2 * GLA_DQK + (h + 1) * GLA_DV)
        rs = slice(2 * GLA_DQK + GLA_DVT + h * GLA_DV, 2 * GLA_DQK + GLA_DVT + (h + 1) * GLA_DV)
        os_ = slice(h * GLA_DV, (h + 1) * GLA_DV)
        bh = b[:, cs]
        v = proj_ref[:, vs].astype(BF16)
        k = proj_ref[:, GLA_DQK + h * GLA_DK:GLA_DQK + (h + 1) * GLA_DK]
        s_old = s_ref[0, h]
        qe = (proj_ref[:, cs] * scale * jnp.exp(bh)).astype(BF16)
        o = _dot(qe, s_old.astype(BF16)) + _dot(a_scr[h].astype(BF16), v)
        kd = (k * jnp.exp(b_last[:, cs] - bh)).astype(BF16)
        bcol = _dot_tn(lf_hi[:, cs], ones_c) + _dot_tn(lf_lo[:, cs], ones_c)
        dec = jnp.exp(bcol)
        s_new = s_old * jnp.concatenate([dec, dec], axis=1) + _dot_tn(kd, v)
        s_ref[0, h] = s_new
        on = o * lax.rsqrt(jnp.mean(o * o, axis=-1, keepdims=True) + EPS) * gon_ref[:, os_]
        r = proj_ref[:, rs]
        og_ref[:, os_] = (on * (r * (1.0 / (1.0 + jnp.exp(-r))))).astype(BF16)


def _gla_scan(proj, wb, bg, gon, s0, batch, t_len):
    c = min(t_len, GLA_MAX_CHUNK)
    nt = t_len // c
    lev_np, levels = _gla_level_matrices(c)
    lev = jnp.asarray(lev_np, BF16)
    tri = jnp.asarray(np.tril(np.ones((c, c), np.float32)), BF16)
    has_s0 = s0 is not None
    in_specs = [
        pl.BlockSpec((c, GLA_PROJ_COLS), lambda b, t: (b * nt + t, 0)),
        pl.BlockSpec((GLA_GATE_PAD, GLA_DQK), lambda b, t: (0, 0)),
        pl.BlockSpec((1, GLA_DQK), lambda b, t: (0, 0)),
        pl.BlockSpec((1, GLA_DVT), lambda b, t: (0, 0)),
        pl.BlockSpec((c, c), lambda b, t: (0, 0)),
        pl.BlockSpec((2 * len(levels), c, c), lambda b, t: (0, 0, 0)),
    ]
    args = [proj, wb, bg, gon, tri, lev]
    if has_s0:
        in_specs.append(pl.BlockSpec((1, GLA_HEADS, GLA_DK, GLA_DV), lambda b, t: (b, 0, 0, 0)))
        args.append(s0)
    og, s_fin = pl.pallas_call(
        functools.partial(_gla_kernel, c=c, levels=tuple(levels), has_s0=has_s0),
        grid=(batch, nt),
        in_specs=in_specs,
        out_specs=[
            pl.BlockSpec((c, GLA_DVT), lambda b, t: (b * nt + t, 0)),
            pl.BlockSpec((1, GLA_HEADS, GLA_DK, GLA_DV), lambda b, t: (b, 0, 0, 0)),
        ],
        out_shape=[
            jax.ShapeDtypeStruct((batch * t_len, GLA_DVT), BF16),
            jax.ShapeDtypeStruct((batch, GLA_HEADS, GLA_DK, GLA_DV), F32),
        ],
        scratch_shapes=[pltpu.VMEM((GLA_HEADS, c, c), F32)],
        compiler_params=_params(("arbitrary", "arbitrary")),
        name="gla_scan",
    )(*args)
    return og, s_fin


def _swa_kernel(sink_ref, qkv_ref, kprev_ref, vprev_ref, o_ref, *, tq, prev_always_valid):
    t = pl.program_id(1)
    kcat = jnp.concatenate([kprev_ref[...], qkv_ref[:, SWA_DQ:SWA_DQ + SWA_DKV]], axis=0)
    vcat = jnp.concatenate([vprev_ref[...], qkv_ref[:, SWA_DQ + SWA_DKV:]], axis=0)
    nwin = WINDOW + CHUNK
    kidx = lax.broadcasted_iota(jnp.int32, (1, nwin), 1)
    for cq in range(tq // CHUNK):
        lo = cq * CHUNK
        kw = kcat[lo:lo + nwin]
        vw = vcat[lo:lo + nwin]
        if prev_always_valid:
            valid = None
        else:
            valid = jnp.logical_or(kidx + lo >= WINDOW, t > 0)
        outs = []
        for kh in range(SWA_KVH):
            kk = kw[:, kh * SWA_HD:(kh + 1) * SWA_HD].astype(BF16)
            vv = vw[:, kh * SWA_HD:(kh + 1) * SWA_HD].astype(BF16)
            for g in range(SWA_GROUP):
                hd = kh * SWA_GROUP + g
                qg = (qkv_ref[lo:lo + CHUNK, hd * SWA_HD:(hd + 1) * SWA_HD] * (SWA_HD ** -0.5)).astype(BF16)
                s = _dot_nt(qg, kk)
                if valid is not None:
                    s = jnp.where(valid, s, -jnp.inf)
                sink = sink_ref[hd]
                m = jnp.maximum(jnp.max(s, axis=-1, keepdims=True), sink)
                p = jnp.exp(s - m)
                den = jnp.sum(p, axis=-1, keepdims=True) + jnp.exp(sink - m)
                outs.append(_dot((p / den).astype(BF16), vv))
        o_ref[lo:lo + CHUNK, :] = jnp.concatenate(outs, axis=1).astype(BF16)


def _swa_attention(qkv, k_prev, v_prev, sinks, batch, t_len, prev_from_qkv):
    tq = min(t_len, 4 * CHUNK)
    nt = t_len // tq
    if prev_from_qkv:
        per = tq // WINDOW
        kspec = pl.BlockSpec((WINDOW, SWA_DKV),
                             lambda b, t: (jnp.maximum((b * nt + t) * per - 1, 0), SWA_DQ // SWA_DKV))
        vspec = pl.BlockSpec((WINDOW, SWA_DKV),
                             lambda b, t: (jnp.maximum((b * nt + t) * per - 1, 0), SWA_DQ // SWA_DKV + 1))
        k_prev = v_prev = qkv
    else:
        kspec = pl.BlockSpec((WINDOW, SWA_DKV), lambda b, t: (b, 0))
        vspec = pl.BlockSpec((WINDOW, SWA_DKV), lambda b, t: (b, 0))
    return pl.pallas_call(
        functools.partial(_swa_kernel, tq=tq, prev_always_valid=not prev_from_qkv),
        grid=(batch, nt),
        in_specs=[
            pl.BlockSpec(memory_space=pltpu.SMEM),
            pl.BlockSpec((tq, SWA_COLS), lambda b, t: (b * nt + t, 0)),
            kspec,
            vspec,
        ],
        out_specs=pl.BlockSpec((tq, SWA_DQ), lambda b, t: (b * nt + t, 0)),
        out_shape=jax.ShapeDtypeStruct((batch * t_len, SWA_DQ), BF16),
        compiler_params=_params(("arbitrary", "arbitrary")),
        name="swa_attention",
    )(sinks, qkv, k_prev, v_prev)


def _post_kernel(o_ref, x_ref, g_ref, wo_ref, bo_ref, wu_ref, wd_ref, y_ref, *, ff_chunk):
    a = _dot(o_ref[...], wo_ref[...]) + bo_ref[...]
    x1 = x_ref[...] + _rms(a, g_ref[1:2, :])
    h = _rms(x1, g_ref[2:3, :]).astype(BF16)
    acc = None
    for j in range(D_FF // ff_chunk):
        u = _dot(h, wu_ref[:, j * ff_chunk:(j + 1) * ff_chunk])
        u = jnp.maximum(u, 0.0)
        u = (u * u).astype(BF16)
        part = _dot(u, wd_ref[j * ff_chunk:(j + 1) * ff_chunk, :])
        acc = part if acc is None else acc + part
    y_ref[...] = x1 + _rms(acc, g_ref[3:4, :])


def _post_block(o, x, g4, wo, bo, wu, wd, tm):
    n, d = x.shape
    const = lambda i: (0, 0)
    return pl.pallas_call(
        functools.partial(_post_kernel, ff_chunk=1024),
        grid=(n // tm,),
        in_specs=[
            pl.BlockSpec((tm, d), lambda i: (i, 0)),
            pl.BlockSpec((tm, d), lambda i: (i, 0)),
            pl.BlockSpec((4, d), const),
            pl.BlockSpec((d, d), const, pipeline_mode=pl.Buffered(1)),
            pl.BlockSpec((1, d), const),
            pl.BlockSpec((d, D_FF), const, pipeline_mode=pl.Buffered(1)),
            pl.BlockSpec((D_FF, d), const, pipeline_mode=pl.Buffered(1)),
        ],
        out_specs=pl.BlockSpec((tm, d), lambda i: (i, 0)),
        out_shape=jax.ShapeDtypeStruct((n, d), F32),
        compiler_params=_params(("arbitrary",)),
        name="post_block",
    )(o, x, g4, wo, bo, wu, wd)


def _trunk(x, gla_s0, swa_k0, swa_v0, w):
    batch, t_len, d = x.shape
    n = batch * t_len
    tm = min(ROW_TILE, n)
    assert n % tm == 0 and t_len % CHUNK == 0
    xf = x.reshape(n, d)
    proj = _norm_proj(xf, w["norm_g"][0], w["gla_w_ext"], w["gla_b_ext"], tm)
    og, s_fin = _gla_scan(proj, w["gla_wb"], w["gla_bg"], w["gla_gon"], gla_s0, batch, t_len)
    x1 = _post_block(og, xf, w["norm_g"][0], w["gla_wo"], w["zero_bias"], w["wu"][0], w["wd"][0], tm)
    qkv = _norm_proj(x1, w["norm_g"][1], w["swa_wi"], w["swa_bi"], tm)
    if swa_k0 is None:
        oa = _swa_attention(qkv, None, None, w["sinks"], batch, t_len, True)
    else:
        kp = swa_k0.reshape(batch * WINDOW, SWA_DKV)
        vp = swa_v0.reshape(batch * WINDOW, SWA_DKV)
        oa = _swa_attention(qkv, kp, vp, w["sinks"], batch, t_len, False)
    y = _post_block(oa, x1, w["norm_g"][1], w["swa_wo"], w["swa_bo"], w["wu"][1], w["wd"][1], tm)
    qkv3 = qkv.reshape(batch, t_len, SWA_COLS)
    k_new = qkv3[:, :, SWA_DQ:SWA_DQ + SWA_DKV].reshape(batch, t_len, SWA_KVH, SWA_HD)
    v_new = qkv3[:, :, SWA_DQ + SWA_DKV:].reshape(batch, t_len, SWA_KVH, SWA_HD)
    if swa_k0 is None:
        keep = min(WINDOW, t_len)
        k_cache, v_cache = k_new[:, t_len - keep:], v_new[:, t_len - keep:]
    else:
        keep = swa_k0.shape[1]
        k_cache = jnp.concatenate([swa_k0, k_new], axis=1)[:, -keep:]
        v_cache = jnp.concatenate([swa_v0, v_new], axis=1)[:, -keep:]
    return y.reshape(batch, t_len, d), s_fin[None], k_cache[None], v_cache[None]


def kernel(x_prompt, x_sample, state_gla, cache_swa_k, cache_swa_v, gla_w_in, gla_w_gate_a, gla_w_gate_b, gla_b_gate, gla_g_onorm, gla_w_out, swa_w_in, swa_b_in, swa_sinks, swa_w_out, swa_b_out, norm_g, mlp_w_up, mlp_w_down):
    d = D_MODEL
    pad_a = jnp.zeros((d, GLA_GATE_PAD - GLA_GATE_RANK), F32)
    pad_b = jnp.zeros((GLA_GATE_PAD - GLA_GATE_RANK, GLA_DQK), F32)
    w = {
        "norm_g": norm_g,
        "gla_w_ext": jnp.concatenate([gla_w_in[0], gla_w_gate_a[0], pad_a], axis=1).astype(BF16),
        "gla_b_ext": jnp.zeros((1, GLA_PROJ_COLS), F32),
        "gla_wb": jnp.concatenate([gla_w_gate_b[0], pad_b], axis=0).astype(BF16),
        "gla_bg": gla_b_gate[0].reshape(1, GLA_DQK),
        "gla_gon": gla_g_onorm[0].reshape(1, GLA_DVT),
        "gla_wo": gla_w_out[0].astype(BF16),
        "zero_bias": jnp.zeros((1, d), F32),
        "swa_wi": swa_w_in[0].astype(BF16),
        "swa_bi": swa_b_in[0].reshape(1, SWA_COLS),
        "sinks": swa_sinks[0],
        "swa_wo": swa_w_out[0].astype(BF16),
        "swa_bo": swa_b_out[0].reshape(1, d),
        "wu": mlp_w_up.astype(BF16),
        "wd": mlp_w_down.astype(BF16),
    }
    y_p, s_p, k_p, v_p = _trunk(x_prompt, None, None, None, w)
    y_s, s_s, k_s, v_s = _trunk(x_sample, state_gla[0], cache_swa_k[0], cache_swa_v[0], w)
    return (y_p, y_s, s_p, s_s, k_p, v_p, k_s, v_s)
```

```python
import functools

import numpy as np
import jax
import jax.numpy as jnp
from jax import lax
from jax.experimental import pallas as pl
from jax.experimental.pallas import tpu as pltpu

F32 = jnp.float32
BF16 = jnp.bfloat16

D_MODEL = 1024
D_FF = 4 * D_MODEL
EPS = 1e-6
CHUNK = 64
GLA_HEADS = 4
GLA_DK = 128
GLA_DV = 256
GLA_DQK = GLA_HEADS * GLA_DK
GLA_DVT = GLA_HEADS * GLA_DV
GLA_GATE_RANK = 16
GLA_GATE_NORM = 16.0
GLA_GATE_PAD = 128
GLA_PROJ_COLS = 2 * GLA_DQK + 2 * GLA_DVT + GLA_GATE_PAD
GLA_MAX_CHUNK = 256
GLA_SAFE_DECAY = 60.0
SWA_HD = 64
SWA_QH = 16
SWA_KVH = 4
SWA_GROUP = 4
SWA_DQ = SWA_QH * SWA_HD
SWA_DKV = SWA_KVH * SWA_HD
SWA_COLS = SWA_DQ + 2 * SWA_DKV
WINDOW = 128
ROW_TILE = 512
VMEM_LIMIT_BYTES = 56 * 1024 * 1024


def _dot(a, b):
    return jnp.dot(a, b, preferred_element_type=F32)


def _dot_nt(a, b):
    return lax.dot_general(a, b, (((1,), (1,)), ((), ())), preferred_element_type=F32)


def _dot_tn(a, b):
    return lax.dot_general(a, b, (((0,), (0,)), ((), ())), preferred_element_type=F32)


def _rms(x, g):
    return x * lax.rsqrt(jnp.mean(x * x, axis=-1, keepdims=True) + EPS) * g


def _split_bf16(x):
    hi = x.astype(BF16)
    lo = (x - hi.astype(F32)).astype(BF16)
    return hi, lo


def _params(semantics):
    return pltpu.CompilerParams(dimension_semantics=semantics, vmem_limit_bytes=VMEM_LIMIT_BYTES)


def _norm_proj_kernel(x_ref, g_ref, w_ref, b_ref, o_ref):
    h = _rms(x_ref[...], g_ref[0:1, :]).astype(BF16)
    o_ref[...] = _dot(h, w_ref[...]) + b_ref[...]


def _norm_proj(x, g4, w, bias, tm):
    n, d = x.shape
    cols = w.shape[1]
    return pl.pallas_call(
        _norm_proj_kernel,
        grid=(n // tm,),
        in_specs=[
            pl.BlockSpec((tm, d), lambda i: (i, 0)),
            pl.BlockSpec((4, d), lambda i: (0, 0)),
            pl.BlockSpec((d, cols), lambda i: (0, 0)),
            pl.BlockSpec((1, cols), lambda i: (0, 0)),
        ],
        out_specs=pl.BlockSpec((tm, cols), lambda i: (i, 0)),
        out_shape=jax.ShapeDtypeStruct((n, cols), F32),
        compiler_params=_params(("arbitrary",)),
        name="norm_proj",
    )(x, g4, w, bias)


def _gla_level_matrices(c):
    levels = []
    s = c
    while s >= 2:
        levels.append(s)
        s //= 2
    t = np.arange(c)[None, :]
    r = np.arange(c)[:, None]
    mats = []
    for s in levels:
        mid = (r // s) * s + s // 2
        mq = (r >= mid) & (t >= mid) & (t <= r)
        mk = (r < mid) & (t > r) & (t <= mid - 1)
        mats.append(mq)
        mats.append(mk)
    return np.stack(mats).astype(np.float32), levels


def _gla_kernel(*refs, c, levels, has_s0):
    if has_s0:
        proj_ref, wb_ref, bg_ref, gon_ref, tri_ref, lev_ref, s0_ref, og_ref, s_ref, a_scr = refs
    else:
        proj_ref, wb_ref, bg_ref, gon_ref, tri_ref, lev_ref, og_ref, s_ref, a_scr = refs
        s0_ref = None
    t = pl.program_id(1)

    @pl.when(t == 0)
    def _():
        if has_s0:
            s_ref[...] = s0_ref[...]
        else:
            s_ref[...] = jnp.zeros(s_ref.shape, F32)

    ga = proj_ref[:, 2 * GLA_DQK + 2 * GLA_DVT:].astype(BF16)
    gp = _dot(ga, wb_ref[...]) + bg_ref[...]
    lf = (jnp.minimum(gp, 0.0) - jnp.log(1.0 + jnp.exp(-jnp.abs(gp)))) * (1.0 / GLA_GATE_NORM)
    lf_hi, lf_lo = _split_bf16(lf)
    tri = tri_ref[...]
    b = _dot(tri, lf_hi) + _dot(tri, lf_lo)
    b_last = b[c - 1:c, :]
    ones_c = jnp.ones((c, GLA_DK), BF16)
    scale = GLA_DK ** -0.5

    row = lax.broadcasted_iota(jnp.int32, (c, c), 0)
    col = lax.broadcasted_iota(jnp.int32, (c, c), 1)

    safe = jnp.min(b) >= -GLA_SAFE_DECAY

    @pl.when(safe)
    def _():
        for h in range(GLA_HEADS):
            cs = slice(h * GLA_DK, (h + 1) * GLA_DK)
            bh = b[:, cs]
            qe = (proj_ref[:, cs] * scale * jnp.exp(bh)).astype(BF16)
            ke = (proj_ref[:, GLA_DQK + h * GLA_DK:GLA_DQK + (h + 1) * GLA_DK] * jnp.exp(-bh)).astype(BF16)
            a_scr[h] = jnp.where(row >= col, _dot_nt(qe, ke), 0.0)

    @pl.when(jnp.logical_not(safe))
    def _():
        for h in range(GLA_HEADS):
            cs = slice(h * GLA_DK, (h + 1) * GLA_DK)
            q = proj_ref[:, cs] * scale
            k = proj_ref[:, GLA_DQK + h * GLA_DK:GLA_DQK + (h + 1) * GLA_DK]
            a_scr[h] = jnp.where(row == col, _dot_nt(q.astype(BF16), k.astype(BF16)), 0.0)
        for li, s in enumerate(levels):
            mq = lev_ref[2 * li]
            mk = lev_ref[2 * li + 1]
            dq = _dot(mq, lf_hi) + _dot(mq, lf_lo)
            dk = _dot(mk, lf_hi) + _dot(mk, lf_lo)
            half = s // 2
            sh = s.bit_length() - 1
            same = lax.shift_right_logical(row, sh) == lax.shift_right_logical(col, sh)
            mask = same & ((row & (s - 1)) >= half) & ((col & (s - 1)) < half)
            for h in range(GLA_HEADS):
                cs = slice(h * GLA_DK, (h + 1) * GLA_DK)
                qq = (proj_ref[:, cs] * scale * jnp.exp(dq[:, cs])).astype(BF16)
                kk = (proj_ref[:, GLA_DQK + h * GLA_DK:GLA_DQK + (h + 1) * GLA_DK]
                      * jnp.exp(dk[:, cs])).astype(BF16)
                a_scr[h] = a_scr[h] + jnp.where(mask, _dot_nt(qq, kk), 0.0)

    for h in range(GLA_HEADS):
        cs = slice(h * GLA_DK, (h + 1) * GLA_DK)
        vs = slice(2 * GLA_DQK + h * GLA_DV, 2 * GLA_DQK + (h + 1) * GLA_DV)
        rs = slice(2 * GLA_DQK + GLA_DVT + h * GLA_DV, 2 * GLA_DQK + GLA_DVT + (h + 1) * GLA_DV)
        os_ = slice(h * GLA_DV, (h + 1) * GLA_DV)
        bh = b[:, cs]
        v = proj_ref[:, vs].astype(BF16)
        k = proj_ref[:, GLA_DQK + h * GLA_DK:GLA_DQK + (h + 1) * GLA_DK]
        s_old = s_ref[0, h]
        qe = (proj_ref[:, cs] * scale * jnp.exp(bh)).astype(BF16)
        o = _dot(qe, s_old.astype(BF16)) + _dot(a_scr[h].astype(BF16), v)
        kd = (k * jnp.exp(b_last[:, cs] - bh)).astype(BF16)
        bcol = _dot_tn(lf_hi[:, cs], ones_c) + _dot_tn(lf_lo[:, cs], ones_c)
        dec = jnp.exp(bcol)
        s_new = s_old * jnp.concatenate([dec, dec], axis=1) + _dot_tn(kd, v)
        s_ref[0, h] = s_new
        on = o * lax.rsqrt(jnp.mean(o * o, axis=-1, keepdims=True) + EPS) * gon_ref[:, os_]
        r = proj_ref[:, rs]
        og_ref[:, os_] = (on * (r * (1.0 / (1.0 + jnp.exp(-r))))).astype(BF16)


def _gla_scan(proj, wb, bg, gon, s0, batch, t_len):
    c = min(t_len, GLA_MAX_CHUNK)
    nt = t_len // c
    lev_np, levels = _gla_level_matrices(c)
    lev = jnp.asarray(lev_np, BF16)
    tri = jnp.asarray(np.tril(np.ones((c, c), np.float32)), BF16)
    has_s0 = s0 is not None
    in_specs = [
        pl.BlockSpec((c, GLA_PROJ_COLS), lambda b, t: (b * nt + t, 0)),
        pl.BlockSpec((GLA_GATE_PAD, GLA_DQK), lambda b, t: (0, 0)),
        pl.BlockSpec((1, GLA_DQK), lambda b, t: (0, 0)),
        pl.BlockSpec((1, GLA_DVT), lambda b, t: (0, 0)),
        pl.BlockSpec((c, c), lambda b, t: (0, 0)),
        pl.BlockSpec((2 * len(levels), c, c), lambda b, t: (0, 0, 0)),
    ]
    args = [proj, wb, bg, gon, tri, lev]
    if has_s0:
        in_specs.append(pl.BlockSpec((1, GLA_HEADS, GLA_DK, GLA_DV), lambda b, t: (b, 0, 0, 0)))
        args.append(s0)
    og, s_fin = pl.pallas_call(
        functools.partial(_gla_kernel, c=c, levels=tuple(levels), has_s0=has_s0),
        grid=(batch, nt),
        in_specs=in_specs,
        out_specs=[
            pl.BlockSpec((c, GLA_DVT), lambda b, t: (b * nt + t, 0)),
            pl.BlockSpec((1, GLA_HEADS, GLA_DK, GLA_DV), lambda b, t: (b, 0, 0, 0)),
        ],
        out_shape=[
            jax.ShapeDtypeStruct((batch * t_len, GLA_DVT), BF16),
            jax.ShapeDtypeStruct((batch, GLA_HEADS, GLA_DK, GLA_DV), F32),
        ],
        scratch_shapes=[pltpu.VMEM((GLA_HEADS, c, c), F32)],
        compiler_params=_params(("arbitrary", "arbitrary")),
        name="gla_scan",
    )(*args)
    return og, s_fin


SWA_NK = 256
SWA_GW = SWA_GROUP * SWA_HD


def _swa_kernel(sink_ref, qkv_ref, kprev_ref, vprev_ref, o_ref, kblk, vblk, *, tq, prev_always_valid):
    t = pl.program_id(1)

    @pl.when(jnp.logical_and(pl.program_id(0) == 0, t == 0))
    def _():
        kblk[...] = jnp.zeros(kblk.shape, BF16)
        vblk[...] = jnp.zeros(vblk.shape, BF16)

    pad = SWA_NK - WINDOW - tq
    k_parts = [kprev_ref[...], qkv_ref[:, SWA_DQ:SWA_DQ + SWA_DKV]]
    v_parts = [vprev_ref[...], qkv_ref[:, SWA_DQ + SWA_DKV:]]
    if pad:
        k_parts.append(jnp.zeros((pad, SWA_DKV), F32))
        v_parts.append(jnp.zeros((pad, SWA_DKV), F32))
    kt = jnp.concatenate(k_parts, axis=0).T.astype(BF16)
    vc = jnp.concatenate(v_parts, axis=0).astype(BF16)
    for kh in range(SWA_KVH):
        for g in range(SWA_GROUP):
            kblk[kh, g * SWA_HD:(g + 1) * SWA_HD, g * SWA_NK:(g + 1) * SWA_NK] = kt[kh * SWA_HD:(kh + 1) * SWA_HD, :]
            vblk[kh, g * SWA_NK:(g + 1) * SWA_NK, g * SWA_HD:(g + 1) * SWA_HD] = vc[:, kh * SWA_HD:(kh + 1) * SWA_HD]

    qrow = lax.broadcasted_iota(jnp.int32, (tq, SWA_NK), 0)
    kcol = lax.broadcasted_iota(jnp.int32, (tq, SWA_NK), 1)
    lo = lax.shift_right_logical(qrow, 6) * CHUNK
    mask = (kcol >= lo) & (kcol < lo + WINDOW + CHUNK) & (kcol < WINDOW + tq)
    if not prev_always_valid:
        mask = mask & jnp.logical_or(kcol >= WINDOW, t > 0)
    lane = lax.broadcasted_iota(jnp.int32, (tq, SWA_GW), 1)

    for kh in range(SWA_KVH):
        q4 = (qkv_ref[:, kh * SWA_GW:(kh + 1) * SWA_GW] * (SWA_HD ** -0.5)).astype(BF16)
        s = _dot(q4, kblk[kh])
        ps, linv = [], []
        for g in range(SWA_GROUP):
            sg = jnp.where(mask, s[:, g * SWA_NK:(g + 1) * SWA_NK], -jnp.inf)
            sink = sink_ref[kh * SWA_GROUP + g]
            m = jnp.maximum(jnp.max(sg, axis=-1, keepdims=True), sink)
            p = jnp.exp(sg - m)
            linv.append(1.0 / (jnp.sum(p, axis=-1, keepdims=True) + jnp.exp(sink - m)))
            ps.append(p.astype(BF16))
        o = _dot(jnp.concatenate(ps, axis=1), vblk[kh])
        scale = jnp.where(lane < SWA_HD, linv[0],
                          jnp.where(lane < 2 * SWA_HD, linv[1],
                                    jnp.where(lane < 3 * SWA_HD, linv[2], linv[3])))
        o_ref[:, kh * SWA_GW:(kh + 1) * SWA_GW] = (o * scale).astype(BF16)


def _swa_attention(qkv, k_prev, v_prev, sinks, batch, t_len, prev_from_qkv):
    tq = min(t_len, WINDOW)
    nt = t_len // tq
    if prev_from_qkv:
        assert tq == WINDOW
        kspec = pl.BlockSpec((WINDOW, SWA_DKV),
                             lambda b, t: (jnp.maximum(b * nt + t - 1, 0), SWA_DQ // SWA_DKV))
        vspec = pl.BlockSpec((WINDOW, SWA_DKV),
                             lambda b, t: (jnp.maximum(b * nt + t - 1, 0), SWA_DQ // SWA_DKV + 1))
        k_prev = v_prev = qkv
    else:
        kspec = pl.BlockSpec((WINDOW, SWA_DKV), lambda b, t: (b, 0))
        vspec = pl.BlockSpec((WINDOW, SWA_DKV), lambda b, t: (b, 0))
    return pl.pallas_call(
        functools.partial(_swa_kernel, tq=tq, prev_always_valid=not prev_from_qkv),
        grid=(batch, nt),
        in_specs=[
            pl.BlockSpec(memory_space=pltpu.SMEM),
            pl.BlockSpec((tq, SWA_COLS), lambda b, t: (b * nt + t, 0)),
            kspec,
            vspec,
        ],
        out_specs=pl.BlockSpec((tq, SWA_DQ), lambda b, t: (b * nt + t, 0)),
        out_shape=jax.ShapeDtypeStruct((batch * t_len, SWA_DQ), BF16),
        scratch_shapes=[
            pltpu.VMEM((SWA_KVH, SWA_GW, SWA_GROUP * SWA_NK), BF16),
            pltpu.VMEM((SWA_KVH, SWA_GROUP * SWA_NK, SWA_GW), BF16),
        ],
        compiler_params=_params(("arbitrary", "arbitrary")),
        name="swa_attention",
    )(sinks, qkv, k_prev, v_prev)


def _post_kernel(o_ref, x_ref, g_ref, wo_ref, bo_ref, wu_ref, wd_ref, y_ref, *, ff_chunk):
    a = _dot(o_ref[...], wo_ref[...]) + bo_ref[...]
    x1 = x_ref[...] + _rms(a, g_ref[1:2, :])
    h = _rms(x1, g_ref[2:3, :]).astype(BF16)
    acc = None
    for j in range(D_FF // ff_chunk):
        u = _dot(h, wu_ref[:, j * ff_chunk:(j + 1) * ff_chunk])
        u = jnp.maximum(u, 0.0)
        u = (u * u).astype(BF16)
        part = _dot(u, wd_ref[j * ff_chunk:(j + 1) * ff_chunk, :])
        acc = part if acc is None else acc + part
    y_ref[...] = x1 + _rms(acc, g_ref[3:4, :])


def _post_block(o, x, g4, wo, bo, wu, wd, tm):
    n, d = x.shape
    const = lambda i: (0, 0)
    return pl.pallas_call(
        functools.partial(_post_kernel, ff_chunk=1024),
        grid=(n // tm,),
        in_specs=[
            pl.BlockSpec((tm, d), lambda i: (i, 0)),
            pl.BlockSpec((tm, d), lambda i: (i, 0)),
            pl.BlockSpec((4, d), const),
            pl.BlockSpec((d, d), const, pipeline_mode=pl.Buffered(1)),
            pl.BlockSpec((1, d), const),
            pl.BlockSpec((d, D_FF), const, pipeline_mode=pl.Buffered(1)),
            pl.BlockSpec((D_FF, d), const, pipeline_mode=pl.Buffered(1)),
        ],
        out_specs=pl.BlockSpec((tm, d), lambda i: (i, 0)),
        out_shape=jax.ShapeDtypeStruct((n, d), F32),
        compiler_params=_params(("arbitrary",)),
        name="post_block",
    )(o, x, g4, wo, bo, wu, wd)


def _trunk(x, gla_s0, swa_k0, swa_v0, w):
    batch, t_len, d = x.shape
    n = batch * t_len
    tm = min(ROW_TILE, n)
    assert n % tm == 0 and t_len % CHUNK == 0
    xf = x.reshape(n, d)
    proj = _norm_proj(xf, w["norm_g"][0], w["gla_w_ext"], w["gla_b_ext"], tm)
    og, s_fin = _gla_scan(proj, w["gla_wb"], w["gla_bg"], w["gla_gon"], gla_s0, batch, t_len)
    x1 = _post_block(og, xf, w["norm_g"][0], w["gla_wo"], w["zero_bias"], w["wu"][0], w["wd"][0], tm)
    qkv = _norm_proj(x1, w["norm_g"][1], w["swa_wi"], w["swa_bi"], tm)
    if swa_k0 is None:
        oa = _swa_attention(qkv, None, None, w["sinks"], batch, t_len, True)
    else:
        kp = swa_k0.reshape(batch * WINDOW, SWA_DKV)
        vp = swa_v0.reshape(batch * WINDOW, SWA_DKV)
        oa = _swa_attention(qkv, kp, vp, w["sinks"], batch, t_len, False)
    y = _post_block(oa, x1, w["norm_g"][1], w["swa_wo"], w["swa_bo"], w["wu"][1], w["wd"][1], tm)
    qkv3 = qkv.reshape(batch, t_len, SWA_COLS)
    k_new = qkv3[:, :, SWA_DQ:SWA_DQ + SWA_DKV].reshape(batch, t_len, SWA_KVH, SWA_HD)
    v_new = qkv3[:, :, SWA_DQ + SWA_DKV:].reshape(batch, t_len, SWA_KVH, SWA_HD)
    if swa_k0 is None:
        keep = min(WINDOW, t_len)
        k_cache, v_cache = k_new[:, t_len - keep:], v_new[:, t_len - keep:]
    else:
        keep = swa_k0.shape[1]
        k_cache = jnp.concatenate([swa_k0, k_new], axis=1)[:, -keep:]
        v_cache = jnp.concatenate([swa_v0, v_new], axis=1)[:, -keep:]
    return y.reshape(batch, t_len, d), s_fin[None], k_cache[None], v_cache[None]


def kernel(x_prompt, x_sample, state_gla, cache_swa_k, cache_swa_v, gla_w_in, gla_w_gate_a, gla_w_gate_b, gla_b_gate, gla_g_onorm, gla_w_out, swa_w_in, swa_b_in, swa_sinks, swa_w_out, swa_b_out, norm_g, mlp_w_up, mlp_w_down):
    d = D_MODEL
    pad_a = jnp.zeros((d, GLA_GATE_PAD - GLA_GATE_RANK), F32)
    pad_b = jnp.zeros((GLA_GATE_PAD - GLA_GATE_RANK, GLA_DQK), F32)
    w = {
        "norm_g": norm_g,
        "gla_w_ext": jnp.concatenate([gla_w_in[0], gla_w_gate_a[0], pad_a], axis=1).astype(BF16),
        "gla_b_ext": jnp.zeros((1, GLA_PROJ_COLS), F32),
        "gla_wb": jnp.concatenate([gla_w_gate_b[0], pad_b], axis=0).astype(BF16),
        "gla_bg": gla_b_gate[0].reshape(1, GLA_DQK),
        "gla_gon": gla_g_onorm[0].reshape(1, GLA_DVT),
        "gla_wo": gla_w_out[0].astype(BF16),
        "zero_bias": jnp.zeros((1, d), F32),
        "swa_wi": swa_w_in[0].astype(BF16),
        "swa_bi": swa_b_in[0].reshape(1, SWA_COLS),
        "sinks": swa_sinks[0],
        "swa_wo": swa_w_out[0].astype(BF16),
        "swa_bo": swa_b_out[0].reshape(1, d),
        "wu": mlp_w_up.astype(BF16),
        "wd": mlp_w_down.astype(BF16),
    }
    y_p, s_p, k_p, v_p = _trunk(x_prompt, None, None, None, w)
    y_s, s_s, k_s, v_s = _trunk(x_sample, state_gla[0], cache_swa_k[0], cache_swa_v[0], w)
    return (y_p, y_s, s_p, s_s, k_p, v_p, k_s, v_s)
```

```python
import functools
import math

import numpy as np
import jax
import jax.numpy as jnp
from jax import lax
from jax.experimental import pallas as pl
from jax.experimental.pallas import tpu as pltpu

F32 = jnp.float32
BF16 = jnp.bfloat16

D_MODEL = 1024
D_FF = 4 * D_MODEL
EPS = 1e-6
CHUNK = 64
LOG2E = math.log2(math.e)
GLA_HEADS = 4
GLA_DK = 128
GLA_DV = 256
GLA_DQK = GLA_HEADS * GLA_DK
GLA_DVT = GLA_HEADS * GLA_DV
GLA_GATE_RANK = 16
GLA_GATE_NORM = 16.0
GLA_GATE_PAD = 128
GLA_PROJ_COLS = 2 * GLA_DQK + 2 * GLA_DVT + GLA_GATE_PAD
GLA_MAX_CHUNK = 256
GLA_SAFE_DECAY = 60.0
SWA_HD = 64
SWA_QH = 16
SWA_KVH = 4
SWA_GROUP = 4
SWA_DQ = SWA_QH * SWA_HD
SWA_DKV = SWA_KVH * SWA_HD
SWA_COLS = SWA_DQ + 2 * SWA_DKV
SWA_GW = SWA_GROUP * SWA_HD
WINDOW = 128
SWA_NK = 2 * WINDOW
ROW_TILE = 512
VMEM_LIMIT_BYTES = 56 * 1024 * 1024


def _dot(a, b):
    return jnp.dot(a, b, preferred_element_type=F32)


def _dot_nt(a, b):
    return lax.dot_general(a, b, (((1,), (1,)), ((), ())), preferred_element_type=F32)


def _dot_tn(a, b):
    return lax.dot_general(a, b, (((0,), (0,)), ((), ())), preferred_element_type=F32)


def _rms(x, g):
    return x * lax.rsqrt(jnp.mean(x * x, axis=-1, keepdims=True) + EPS) * g


def _split_bf16(x):
    hi = x.astype(BF16)
    lo = (x - hi.astype(F32)).astype(BF16)
    return hi, lo


def _params(semantics):
    return pltpu.CompilerParams(dimension_semantics=semantics, vmem_limit_bytes=VMEM_LIMIT_BYTES)


def _resident(shape):
    zeros = (0,) * len(shape)
    return pl.BlockSpec(shape, lambda *_: zeros, pipeline_mode=pl.Buffered(1))


def _gla_level_matrices(c):
    levels = []
    s = c
    while s >= 2:
        levels.append(s)
        s //= 2
    t = np.arange(c)[None, :]
    r = np.arange(c)[:, None]
    mats = []
    for s in levels:
        mid = (r // s) * s + s // 2
        mq = (r >= mid) & (t >= mid) & (t <= r)
        mk = (r < mid) & (t > r) & (t <= mid - 1)
        mats.append(mq)
        mats.append(mk)
    return np.stack(mats).astype(np.float32), levels


def _gla_kernel(*refs, c, levels, has_s0):
    if has_s0:
        (x_ref, g_ref, w_ref, wb_ref, bg_ref, gon_ref, tri_ref, lev_ref, s0_ref,
         og_ref, s_ref, a_scr, sprev) = refs
    else:
        (x_ref, g_ref, w_ref, wb_ref, bg_ref, gon_ref, tri_ref, lev_ref,
         og_ref, s_ref, a_scr, sprev) = refs
        s0_ref = None

    @pl.when(pl.program_id(1) == 0)
    def _():
        if has_s0:
            s_ref[...] = s0_ref[...]
        else:
            s_ref[...] = jnp.zeros(s_ref.shape, F32)

    scale = GLA_DK ** -0.5
    row = lax.broadcasted_iota(jnp.int32, (c, c), 0)
    col = lax.broadcasted_iota(jnp.int32, (c, c), 1)
    nb = min(c, GLA_DK)

    def step(robust):
        h = _rms(x_ref[...], g_ref[0:1, :]).astype(BF16)
        qk = _dot(h, w_ref[:, :2 * GLA_DQK])
        v_all = _dot(h, w_ref[:, 2 * GLA_DQK:2 * GLA_DQK + GLA_DVT]).astype(BF16)
        r_all = _dot(h, w_ref[:, 2 * GLA_DQK + GLA_DVT:2 * GLA_DQK + 2 * GLA_DVT])
        ga = _dot(h, w_ref[:, 2 * GLA_DQK + 2 * GLA_DVT:]).astype(BF16)
        gp = _dot(ga, wb_ref[...]) + bg_ref[...]
        lf = (jnp.minimum(gp, 0.0) - jnp.log(1.0 + jnp.exp(-jnp.abs(gp)))) * (1.0 / GLA_GATE_NORM)
        lf_hi, lf_lo = _split_bf16(lf)
        tri = tri_ref[...]
        b = _dot(tri, lf_hi) + _dot(tri, lf_lo)
        b_last = b[c - 1:c, :]

        if robust:
            for hd in range(GLA_HEADS):
                cs = slice(hd * GLA_DK, (hd + 1) * GLA_DK)
                ks = slice(GLA_DQK + hd * GLA_DK, GLA_DQK + (hd + 1) * GLA_DK)
                a_scr[hd] = jnp.where(row == col,
                                      _dot_nt((qk[:, cs] * scale).astype(BF16), qk[:, ks].astype(BF16)), 0.0)
            for li, s in enumerate(levels):
                mq = lev_ref[2 * li]
                mk = lev_ref[2 * li + 1]
                dq = _dot(mq, lf_hi) + _dot(mq, lf_lo)
                dk = _dot(mk, lf_hi) + _dot(mk, lf_lo)
                half = s // 2
                sh = s.bit_length() - 1
                same = lax.shift_right_logical(row, sh) == lax.shift_right_logical(col, sh)
                mask = same & ((row & (s - 1)) >= half) & ((col & (s - 1)) < half)
                for hd in range(GLA_HEADS):
                    cs = slice(hd * GLA_DK, (hd + 1) * GLA_DK)
                    ks = slice(GLA_DQK + hd * GLA_DK, GLA_DQK + (hd + 1) * GLA_DK)
                    qq = (qk[:, cs] * scale * jnp.exp(dq[:, cs])).astype(BF16)
                    kk = (qk[:, ks] * jnp.exp(dk[:, cs])).astype(BF16)
                    a_scr[hd] = a_scr[hd] + jnp.where(mask, _dot_nt(qq, kk), 0.0)

        for hd in range(GLA_HEADS):
            cs = slice(hd * GLA_DK, (hd + 1) * GLA_DK)
            ks = slice(GLA_DQK + hd * GLA_DK, GLA_DQK + (hd + 1) * GLA_DK)
            vs = slice(hd * GLA_DV, (hd + 1) * GLA_DV)
            bh = b[:, cs]
            v = v_all[:, vs]
            k = qk[:, ks]
            qe = (qk[:, cs] * scale * jnp.exp(bh)).astype(BF16)
            if robust:
                s_old = sprev[hd]
                a = a_scr[hd].astype(BF16)
            else:
                s_old = s_ref[0, hd]
                sprev[hd] = s_old
                ke = (k * jnp.exp(-bh)).astype(BF16)
                a = jnp.where(row >= col, _dot_nt(qe, ke), 0.0).astype(BF16)
                kd = (k * jnp.exp(b_last[:, cs] - bh)).astype(BF16)
                dec = jnp.exp(jnp.transpose(b[c - nb:c, cs])[:, nb - 1:nb])
                s_ref[0, hd] = s_old * dec + _dot_tn(kd, v)
            o = _dot(qe, s_old.astype(BF16)) + _dot(a, v)
            on = o * lax.rsqrt(jnp.mean(o * o, axis=-1, keepdims=True) + EPS) * gon_ref[:, vs]
            hr = 0.5 * r_all[:, vs]
            og_ref[:, vs] = (on * (hr + hr * jnp.tanh(hr))).astype(BF16)
        return jnp.min(b)

    b_min = step(False)

    @pl.when(b_min < -GLA_SAFE_DECAY)
    def _():
        step(True)


def _gla_layer(x, g4, w_ext, wb, bg, gon, s0, batch, t_len):
    d = x.shape[1]
    c = min(t_len, GLA_MAX_CHUNK)
    nt = t_len // c
    lev_np, levels = _gla_level_matrices(c)
    lev = jnp.asarray(lev_np, BF16)
    tri = jnp.asarray(np.tril(np.ones((c, c), np.float32)), BF16)
    has_s0 = s0 is not None
    in_specs = [
        pl.BlockSpec((c, d), lambda b, t: (b * nt + t, 0)),
        _resident((4, d)),
        _resident((d, GLA_PROJ_COLS)),
        _resident((GLA_GATE_PAD, GLA_DQK)),
        _resident((1, GLA_DQK)),
        _resident((1, GLA_DVT)),
        _resident((c, c)),
        _resident((2 * len(levels), c, c)),
    ]
    args = [x, g4, w_ext, wb, bg, gon, tri, lev]
    if has_s0:
        in_specs.append(pl.BlockSpec((1, GLA_HEADS, GLA_DK, GLA_DV), lambda b, t: (b, 0, 0, 0)))
        args.append(s0)
    og, s_fin = pl.pallas_call(
        functools.partial(_gla_kernel, c=c, levels=tuple(levels), has_s0=has_s0),
        grid=(batch, nt),
        in_specs=in_specs,
        out_specs=[
            pl.BlockSpec((c, GLA_DVT), lambda b, t: (b * nt + t, 0)),
            pl.BlockSpec((1, GLA_HEADS, GLA_DK, GLA_DV), lambda b, t: (b, 0, 0, 0)),
        ],
        out_shape=[
            jax.ShapeDtypeStruct((batch * t_len, GLA_DVT), BF16),
            jax.ShapeDtypeStruct((batch, GLA_HEADS, GLA_DK, GLA_DV), F32),
        ],
        scratch_shapes=[
            pltpu.VMEM((GLA_HEADS, c, c), F32),
            pltpu.VMEM((GLA_HEADS, GLA_DK, GLA_DV), F32),
        ],
        compiler_params=_params(("arbitrary", "arbitrary")),
        name="gla_layer",
    )(*args)
    return og, s_fin


def _swa_kernel(sink_ref, qkv_ref, kprev_ref, vprev_ref, o_ref, kblk, vblk, bias_scr, *, tq, prev_always_valid):
    t = pl.program_id(1)

    @pl.when(jnp.logical_and(pl.program_id(0) == 0, t == 0))
    def _():
        kblk[...] = jnp.zeros(kblk.shape, BF16)
        vblk[...] = jnp.zeros(vblk.shape, BF16)

    pad = SWA_NK - WINDOW - tq
    k_parts = [kprev_ref[...], qkv_ref[:, SWA_DQ:SWA_DQ + SWA_DKV]]
    v_parts = [vprev_ref[...], qkv_ref[:, SWA_DQ + SWA_DKV:]]
    if pad:
        k_parts.append(jnp.zeros((pad, SWA_DKV), F32))
        v_parts.append(jnp.zeros((pad, SWA_DKV), F32))
    kt = jnp.concatenate(k_parts, axis=0).T.astype(BF16)
    vc = jnp.concatenate(v_parts, axis=0).astype(BF16)
    for kh in range(SWA_KVH):
        for g in range(SWA_GROUP):
            kblk[kh, g * SWA_HD:(g + 1) * SWA_HD, g * SWA_NK:(g + 1) * SWA_NK] = kt[kh * SWA_HD:(kh + 1) * SWA_HD, :]
            vblk[kh, g * SWA_NK:(g + 1) * SWA_NK, g * SWA_HD:(g + 1) * SWA_HD] = vc[:, kh * SWA_HD:(kh + 1) * SWA_HD]

    qrow = lax.broadcasted_iota(jnp.int32, (tq, SWA_NK), 0)
    kcol = lax.broadcasted_iota(jnp.int32, (tq, SWA_NK), 1)
    lo = lax.shift_right_logical(qrow, 6) * CHUNK
    mask = (kcol >= lo) & (kcol < lo + WINDOW + CHUNK) & (kcol < WINDOW + tq)
    if not prev_always_valid:
        mask = mask & jnp.logical_or(kcol >= WINDOW, t > 0)
    bias_scr[...] = jnp.where(mask, 0.0, -jnp.inf)
    lane = lax.broadcasted_iota(jnp.int32, (tq, SWA_GW), 1)

    for kh in range(SWA_KVH):
        q4 = (qkv_ref[:, kh * SWA_GW:(kh + 1) * SWA_GW] * (SWA_HD ** -0.5 * LOG2E)).astype(BF16)
        acc = None
        linv = []
        for g in range(SWA_GROUP):
            sg = _dot(q4, kblk[kh, :, g * SWA_NK:(g + 1) * SWA_NK]) + bias_scr[...]
            sink = sink_ref[kh * SWA_GROUP + g] * LOG2E
            m = jnp.maximum(jnp.max(sg, axis=-1, keepdims=True), sink)
            p = jnp.exp2(sg - m)
            linv.append(1.0 / (jnp.sum(p, axis=-1, keepdims=True) + jnp.exp2(sink - m)))
            part = _dot(p.astype(BF16), vblk[kh, g * SWA_NK:(g + 1) * SWA_NK, :])
            acc = part if acc is None else acc + part
        norm = jnp.where(lane < SWA_HD, linv[0],
                         jnp.where(lane < 2 * SWA_HD, linv[1],
                                   jnp.where(lane < 3 * SWA_HD, linv[2], linv[3])))
        o_ref[:, kh * SWA_GW:(kh + 1) * SWA_GW] = (acc * norm).astype(BF16)


def _swa_attention(qkv, k_prev, v_prev, sinks, batch, t_len, prev_from_qkv):
    tq = min(t_len, WINDOW)
    nt = t_len // tq
    if prev_from_qkv:
        assert tq == WINDOW
        kspec = pl.BlockSpec((WINDOW, SWA_DKV),
                             lambda b, t: (jnp.maximum(b * nt + t - 1, 0), SWA_DQ // SWA_DKV))
        vspec = pl.BlockSpec((WINDOW, SWA_DKV),
                             lambda b, t: (jnp.maximum(b * nt + t - 1, 0), SWA_DQ // SWA_DKV + 1))
        k_prev = v_prev = qkv
    else:
        kspec = pl.BlockSpec((WINDOW, SWA_DKV), lambda b, t: (b, 0))
        vspec = pl.BlockSpec((WINDOW, SWA_DKV), lambda b, t: (b, 0))
    return pl.pallas_call(
        functools.partial(_swa_kernel, tq=tq, prev_always_valid=not prev_from_qkv),
        grid=(batch, nt),
        in_specs=[
            pl.BlockSpec(memory_space=pltpu.SMEM),
            pl.BlockSpec((tq, SWA_COLS), lambda b, t: (b * nt + t, 0)),
            kspec,
            vspec,
        ],
        out_specs=pl.BlockSpec((tq, SWA_DQ), lambda b, t: (b * nt + t, 0)),
        out_shape=jax.ShapeDtypeStruct((batch * t_len, SWA_DQ), BF16),
        scratch_shapes=[
            pltpu.VMEM((SWA_KVH, SWA_GW, SWA_GROUP * SWA_NK), BF16),
            pltpu.VMEM((SWA_KVH, SWA_GROUP * SWA_NK, SWA_GW), BF16),
            pltpu.VMEM((tq, SWA_NK), F32),
        ],
        compiler_params=_params(("arbitrary", "arbitrary")),
        name="swa_attention",
    )(sinks, qkv, k_prev, v_prev)


def _post_kernel(*refs, ff_chunk, with_next):
    if with_next:
        o_ref, x_ref, g_ref, wo_ref, bo_ref, wu_ref, wd_ref, gn_ref, wn_ref, bn_ref, y_ref, p_ref = refs
    else:
        o_ref, x_ref, g_ref, wo_ref, bo_ref, wu_ref, wd_ref, y_ref = refs
    a = _dot(o_ref[...], wo_ref[...]) + bo_ref[...]
    x1 = x_ref[...] + _rms(a, g_ref[1:2, :])
    h = _rms(x1, g_ref[2:3, :]).astype(BF16)
    acc = None
    for j in range(D_FF // ff_chunk):
        u = _dot(h, wu_ref[:, j * ff_chunk:(j + 1) * ff_chunk])
        u = jnp.maximum(u, 0.0)
        u = (u * u).astype(BF16)
        part = _dot(u, wd_ref[j * ff_chunk:(j + 1) * ff_chunk, :])
        acc = part if acc is None else acc + part
    y = x1 + _rms(acc, g_ref[3:4, :])
    y_ref[...] = y
    if with_next:
        hn = _rms(y, gn_ref[0:1, :]).astype(BF16)
        p_ref[...] = _dot(hn, wn_ref[...]) + bn_ref[...]


def _post_block(o, x, g4, wo, bo, wu, wd, tm, nxt=None):
    n, d = x.shape
    rows = lambda i: (i, 0)
    in_specs = [
        pl.BlockSpec((tm, d), rows),
        pl.BlockSpec((tm, d), rows),
        _resident((4, d)),
        _resident((d, d)),
        _resident((1, d)),
        _resident((d, D_FF)),
        _resident((D_FF, d)),
    ]
    args = [o, x, g4, wo, bo, wu, wd]
    out_specs = [pl.BlockSpec((tm, d), rows)]
    out_shape = [jax.ShapeDtypeStruct((n, d), F32)]
    if nxt is not None:
        gn, wn, bn = nxt
        cols = wn.shape[1]
        in_specs += [_resident((4, d)), _resident((d, cols)), _resident((1, cols))]
        args += [gn, wn, bn]
        out_specs.append(pl.BlockSpec((tm, cols), rows))
        out_shape.append(jax.ShapeDtypeStruct((n, cols), F32))
    outs = pl.pallas_call(
        functools.partial(_post_kernel, ff_chunk=1024, with_next=nxt is not None),
        grid=(n // tm,),
        in_specs=in_specs,
        out_specs=out_specs,
        out_shape=out_shape,
        compiler_params=_params(("arbitrary",)),
        name="post_block",
    )(*args)
    return outs if nxt is not None else outs[0]


def _trunk(x, gla_s0, swa_k0, swa_v0, w):
    batch, t_len, d = x.shape
    n = batch * t_len
    tm = min(ROW_TILE, n)
    assert n % tm == 0 and t_len % CHUNK == 0
    xf = x.reshape(n, d)
    og, s_fin = _gla_layer(xf, w["norm_g"][0], w["gla_w_ext"], w["gla_wb"], w["gla_bg"], w["gla_gon"],
                           gla_s0, batch, t_len)
    x1, qkv = _post_block(og, xf, w["norm_g"][0], w["gla_wo"], w["zero_bias"], w["wu"][0], w["wd"][0], tm,
                          nxt=(w["norm_g"][1], w["swa_wi"], w["swa_bi"]))
    if swa_k0 is None:
        oa = _swa_attention(qkv, None, None, w["sinks"], batch, t_len, True)
    else:
        kp = swa_k0.reshape(batch * WINDOW, SWA_DKV)
        vp = swa_v0.reshape(batch * WINDOW, SWA_DKV)
        oa = _swa_attention(qkv, kp, vp, w["sinks"], batch, t_len, False)
    y = _post_block(oa, x1, w["norm_g"][1], w["swa_wo"], w["swa_bo"], w["wu"][1], w["wd"][1], tm)
    qkv3 = qkv.reshape(batch, t_len, SWA_COLS)
    k_new = qkv3[:, :, SWA_DQ:SWA_DQ + SWA_DKV].reshape(batch, t_len, SWA_KVH, SWA_HD)
    v_new = qkv3[:, :, SWA_DQ + SWA_DKV:].reshape(batch, t_len, SWA_KVH, SWA_HD)
    if swa_k0 is None:
        keep = min(WINDOW, t_len)
        k_cache, v_cache = k_new[:, t_len - keep:], v_new[:, t_len - keep:]
    else:
        keep = swa_k0.shape[1]
        k_cache = jnp.concatenate([swa_k0, k_new], axis=1)[:, -keep:]
        v_cache = jnp.concatenate([swa_v0, v_new], axis=1)[:, -keep:]
    return y.reshape(batch, t_len, d), s_fin[None], k_cache[None], v_cache[None]


def kernel(x_prompt, x_sample, state_gla, cache_swa_k, cache_swa_v, gla_w_in, gla_w_gate_a, gla_w_gate_b, gla_b_gate, gla_g_onorm, gla_w_out, swa_w_in, swa_b_in, swa_sinks, swa_w_out, swa_b_out, norm_g, mlp_w_up, mlp_w_down):
    d = D_MODEL
    pad_a = jnp.zeros((d, GLA_GATE_PAD - GLA_GATE_RANK), F32)
    pad_b = jnp.zeros((GLA_GATE_PAD - GLA_GATE_RANK, GLA_DQK), F32)
    w = {
        "norm_g": norm_g,
        "gla_w_ext": jnp.concatenate([gla_w_in[0], gla_w_gate_a[0], pad_a], axis=1).astype(BF16),
        "gla_wb": jnp.concatenate([gla_w_gate_b[0], pad_b], axis=0).astype(BF16),
        "gla_bg": gla_b_gate[0].reshape(1, GLA_DQK),
        "gla_gon": gla_g_onorm[0].reshape(1, GLA_DVT),
        "gla_wo": gla_w_out[0].astype(BF16),
        "zero_bias": jnp.zeros((1, d), F32),
        "swa_wi": swa_w_in[0].astype(BF16),
        "swa_bi": swa_b_in[0].reshape(1, SWA_COLS),
        "sinks": swa_sinks[0],
        "swa_wo": swa_w_out[0].astype(BF16),
        "swa_bo": swa_b_out[0].reshape(1, d),
        "wu": mlp_w_up.astype(BF16),
        "wd": mlp_w_down.astype(BF16),
    }
    y_p, s_p, k_p, v_p = _trunk(x_prompt, None, None, None, w)
    y_s, s_s, k_s, v_s = _trunk(x_sample, state_gla[0], cache_swa_k[0], cache_swa_v[0], w)
    return (y_p, y_s, s_p, s_s, k_p, v_p, k_s, v_s)
```

```python
import functools
import math

import numpy as np
import jax
import jax.numpy as jnp
from jax import lax
from jax.experimental import pallas as pl
from jax.experimental.pallas import tpu as pltpu

F32 = jnp.float32
BF16 = jnp.bfloat16

D_MODEL = 1024
D_FF = 4 * D_MODEL
EPS = 1e-6
CHUNK = 64
LOG2E = math.log2(math.e)
GLA_HEADS = 4
GLA_DK = 128
GLA_DV = 256
GLA_DQK = GLA_HEADS * GLA_DK
GLA_DVT = GLA_HEADS * GLA_DV
GLA_GATE_RANK = 16
GLA_GATE_NORM = 16.0
GLA_GATE_PAD = 128
GLA_PROJ_COLS = 2 * GLA_DQK + 2 * GLA_DVT + GLA_GATE_PAD
GLA_MAX_CHUNK = 256
GLA_SAFE_DECAY = 60.0
SWA_HD = 64
SWA_QH = 16
SWA_KVH = 4
SWA_GROUP = 4
SWA_DQ = SWA_QH * SWA_HD
SWA_DKV = SWA_KVH * SWA_HD
SWA_COLS = SWA_DQ + 2 * SWA_DKV
SWA_GW = SWA_GROUP * SWA_HD
WINDOW = 128
SWA_NK = 2 * WINDOW
ROW_TILE = 512
SWA_LAYER_TILE = 512
VMEM_LIMIT_BYTES = 60 * 1024 * 1024


def _dot(a, b):
    return jnp.dot(a, b, preferred_element_type=F32)


def _dot_nt(a, b):
    return lax.dot_general(a, b, (((1,), (1,)), ((), ())), preferred_element_type=F32)


def _dot_tn(a, b):
    return lax.dot_general(a, b, (((0,), (0,)), ((), ())), preferred_element_type=F32)


def _rms(x, g):
    return x * lax.rsqrt(jnp.mean(x * x, axis=-1, keepdims=True) + EPS) * g


def _split_bf16(x):
    hi = x.astype(BF16)
    lo = (x - hi.astype(F32)).astype(BF16)
    return hi, lo


def _params(semantics):
    return pltpu.CompilerParams(dimension_semantics=semantics, vmem_limit_bytes=VMEM_LIMIT_BYTES)


def _resident(shape):
    zeros = (0,) * len(shape)
    return pl.BlockSpec(shape, lambda *_: zeros, pipeline_mode=pl.Buffered(1))


def _gla_level_matrices(c):
    levels = []
    s = c
    while s >= 2:
        levels.append(s)
        s //= 2
    t = np.arange(c)[None, :]
    r = np.arange(c)[:, None]
    mats = []
    for s in levels:
        mid = (r // s) * s + s // 2
        mq = (r >= mid) & (t >= mid) & (t <= r)
        mk = (r < mid) & (t > r) & (t <= mid - 1)
        mats.append(mq)
        mats.append(mk)
    return np.stack(mats).astype(np.float32), levels


def _gla_kernel(*refs, c, levels, has_s0):
    if has_s0:
        (x_ref, g_ref, w_ref, wb_ref, bg_ref, gon_ref, tri_ref, lev_ref, s0_ref,
         og_ref, s_ref, a_scr, sprev) = refs
    else:
        (x_ref, g_ref, w_ref, wb_ref, bg_ref, gon_ref, tri_ref, lev_ref,
         og_ref, s_ref, a_scr, sprev) = refs
        s0_ref = None

    @pl.when(pl.program_id(1) == 0)
    def _():
        if has_s0:
            s_ref[...] = s0_ref[...]
        else:
            s_ref[...] = jnp.zeros(s_ref.shape, F32)

    scale = GLA_DK ** -0.5
    row = lax.broadcasted_iota(jnp.int32, (c, c), 0)
    col = lax.broadcasted_iota(jnp.int32, (c, c), 1)
    nb = min(c, GLA_DK)

    def step(robust):
        h = _rms(x_ref[...], g_ref[0:1, :]).astype(BF16)
        qk = _dot(h, w_ref[:, :2 * GLA_DQK])
        v_all = _dot(h, w_ref[:, 2 * GLA_DQK:2 * GLA_DQK + GLA_DVT]).astype(BF16)
        r_all = _dot(h, w_ref[:, 2 * GLA_DQK + GLA_DVT:2 * GLA_DQK + 2 * GLA_DVT])
        ga = _dot(h, w_ref[:, 2 * GLA_DQK + 2 * GLA_DVT:]).astype(BF16)
        gp = _dot(ga, wb_ref[...]) + bg_ref[...]
        lf = (jnp.minimum(gp, 0.0) - jnp.log(1.0 + jnp.exp(-jnp.abs(gp)))) * (1.0 / GLA_GATE_NORM)
        lf_hi, lf_lo = _split_bf16(lf)
        tri = tri_ref[...]
        b = _dot(tri, lf_hi) + _dot(tri, lf_lo)
        b_last = b[c - 1:c, :]

        if robust:
            for hd in range(GLA_HEADS):
                cs = slice(hd * GLA_DK, (hd + 1) * GLA_DK)
                ks = slice(GLA_DQK + hd * GLA_DK, GLA_DQK + (hd + 1) * GLA_DK)
                a_scr[hd] = jnp.where(row == col,
                                      _dot_nt((qk[:, cs] * scale).astype(BF16), qk[:, ks].astype(BF16)), 0.0)
            for li, s in enumerate(levels):
                mq = lev_ref[2 * li]
                mk = lev_ref[2 * li + 1]
                dq = _dot(mq, lf_hi) + _dot(mq, lf_lo)
                dk = _dot(mk, lf_hi) + _dot(mk, lf_lo)
                half = s // 2
                sh = s.bit_length() - 1
                same = lax.shift_right_logical(row, sh) == lax.shift_right_logical(col, sh)
                mask = same & ((row & (s - 1)) >= half) & ((col & (s - 1)) < half)
                for hd in range(GLA_HEADS):
                    cs = slice(hd * GLA_DK, (hd + 1) * GLA_DK)
                    ks = slice(GLA_DQK + hd * GLA_DK, GLA_DQK + (hd + 1) * GLA_DK)
                    qq = (qk[:, cs] * scale * jnp.exp(dq[:, cs])).astype(BF16)
                    kk = (qk[:, ks] * jnp.exp(dk[:, cs])).astype(BF16)
                    a_scr[hd] = a_scr[hd] + jnp.where(mask, _dot_nt(qq, kk), 0.0)

        for hd in range(GLA_HEADS):
            cs = slice(hd * GLA_DK, (hd + 1) * GLA_DK)
            ks = slice(GLA_DQK + hd * GLA_DK, GLA_DQK + (hd + 1) * GLA_DK)
            vs = slice(hd * GLA_DV, (hd + 1) * GLA_DV)
            bh = b[:, cs]
            v = v_all[:, vs]
            k = qk[:, ks]
            qe = (qk[:, cs] * scale * jnp.exp(bh)).astype(BF16)
            if robust:
                s_old = sprev[hd]
                a = a_scr[hd].astype(BF16)
            else:
                s_old = s_ref[0, hd]
                sprev[hd] = s_old
                ke = (k * jnp.exp(-bh)).astype(BF16)
                a = jnp.where(row >= col, _dot_nt(qe, ke), 0.0).astype(BF16)
                kd = (k * jnp.exp(b_last[:, cs] - bh)).astype(BF16)
                dec = jnp.exp(jnp.transpose(b[c - nb:c, cs])[:, nb - 1:nb])
                s_ref[0, hd] = s_old * dec + _dot_tn(kd, v)
            o = _dot(qe, s_old.astype(BF16)) + _dot(a, v)
            on = o * lax.rsqrt(jnp.mean(o * o, axis=-1, keepdims=True) + EPS) * gon_ref[:, vs]
            hr = 0.5 * r_all[:, vs]
            og_ref[:, vs] = (on * (hr + hr * jnp.tanh(hr))).astype(BF16)
        return jnp.min(b)

    b_min = step(False)

    @pl.when(b_min < -GLA_SAFE_DECAY)
    def _():
        step(True)


def _gla_layer(x, g4, w_ext, wb, bg, gon, s0, batch, t_len):
    d = x.shape[1]
    c = min(t_len, GLA_MAX_CHUNK)
    nt = t_len // c
    lev_np, levels = _gla_level_matrices(c)
    lev = jnp.asarray(lev_np, BF16)
    tri = jnp.asarray(np.tril(np.ones((c, c), np.float32)), BF16)
    has_s0 = s0 is not None
    in_specs = [
        pl.BlockSpec((c, d), lambda b, t: (b * nt + t, 0)),
        _resident((4, d)),
        _resident((d, GLA_PROJ_COLS)),
        _resident((GLA_GATE_PAD, GLA_DQK)),
        _resident((1, GLA_DQK)),
        _resident((1, GLA_DVT)),
        _resident((c, c)),
        _resident((2 * len(levels), c, c)),
    ]
    args = [x, g4, w_ext, wb, bg, gon, tri, lev]
    if has_s0:
        in_specs.append(pl.BlockSpec((1, GLA_HEADS, GLA_DK, GLA_DV), lambda b, t: (b, 0, 0, 0)))
        args.append(s0)
    og, s_fin = pl.pallas_call(
        functools.partial(_gla_kernel, c=c, levels=tuple(levels), has_s0=has_s0),
        grid=(batch, nt),
        in_specs=in_specs,
        out_specs=[
            pl.BlockSpec((c, GLA_DVT), lambda b, t: (b * nt + t, 0)),
            pl.BlockSpec((1, GLA_HEADS, GLA_DK, GLA_DV), lambda b, t: (b, 0, 0, 0)),
        ],
        out_shape=[
            jax.ShapeDtypeStruct((batch * t_len, GLA_DVT), BF16),
            jax.ShapeDtypeStruct((batch, GLA_HEADS, GLA_DK, GLA_DV), F32),
        ],
        scratch_shapes=[
            pltpu.VMEM((GLA_HEADS, c, c), F32),
            pltpu.VMEM((GLA_HEADS, GLA_DK, GLA_DV), F32),
        ],
        compiler_params=_params(("arbitrary", "arbitrary")),
        name="gla_layer",
    )(*args)
    return og, s_fin


def _attend(qkv_ref, r0, tq, k_prev, v_prev, prev_valid, sink_ref, kblk, vblk, bias_scr, o_ref):
    for piece in _attend_pieces(qkv_ref, r0, tq, k_prev, v_prev, prev_valid, sink_ref, kblk, vblk, bias_scr, o_ref):
        piece()


def _attend_pieces(qkv_ref, r0, tq, k_prev, v_prev, prev_valid, sink_ref, kblk, vblk, bias_scr, o_ref):
    def prologue():
        pad = SWA_NK - WINDOW - tq
        k_parts = [k_prev(), qkv_ref[r0:r0 + tq, SWA_DQ:SWA_DQ + SWA_DKV]]
        v_parts = [v_prev(), qkv_ref[r0:r0 + tq, SWA_DQ + SWA_DKV:]]
        if pad:
            k_parts.append(jnp.zeros((pad, SWA_DKV), F32))
            v_parts.append(jnp.zeros((pad, SWA_DKV), F32))
        kt = jnp.concatenate(k_parts, axis=0).T.astype(BF16)
        vc = jnp.concatenate(v_parts, axis=0).astype(BF16)
        for kh in range(SWA_KVH):
            for g in range(SWA_GROUP):
                kblk[kh, g * SWA_HD:(g + 1) * SWA_HD, g * SWA_NK:(g + 1) * SWA_NK] = kt[kh * SWA_HD:(kh + 1) * SWA_HD, :]
                vblk[kh, g * SWA_NK:(g + 1) * SWA_NK, g * SWA_HD:(g + 1) * SWA_HD] = vc[:, kh * SWA_HD:(kh + 1) * SWA_HD]
        qrow = lax.broadcasted_iota(jnp.int32, (tq, SWA_NK), 0)
        kcol = lax.broadcasted_iota(jnp.int32, (tq, SWA_NK), 1)
        lo = lax.shift_right_logical(qrow, 6) * CHUNK
        mask = (kcol >= lo) & (kcol < lo + WINDOW + CHUNK) & (kcol < WINDOW + tq)
        if prev_valid is not None:
            mask = mask & jnp.logical_or(kcol >= WINDOW, prev_valid)
        bias_scr[...] = jnp.where(mask, 0.0, -jnp.inf)

    def head(kh):
        lane = lax.broadcasted_iota(jnp.int32, (tq, SWA_GW), 1)
        q4 = (qkv_ref[r0:r0 + tq, kh * SWA_GW:(kh + 1) * SWA_GW] * (SWA_HD ** -0.5 * LOG2E)).astype(BF16)
        acc = None
        linv = []
        for g in range(SWA_GROUP):
            sg = _dot(q4, kblk[kh, :, g * SWA_NK:(g + 1) * SWA_NK]) + bias_scr[...]
            sink = sink_ref[kh * SWA_GROUP + g] * LOG2E
            m = jnp.maximum(jnp.max(sg, axis=-1, keepdims=True), sink)
            p = jnp.exp2(sg - m)
            linv.append(1.0 / (jnp.sum(p, axis=-1, keepdims=True) + jnp.exp2(sink - m)))
            part = _dot(p.astype(BF16), vblk[kh, g * SWA_NK:(g + 1) * SWA_NK, :])
            acc = part if acc is None else acc + part
        norm = jnp.where(lane < SWA_HD, linv[0],
                         jnp.where(lane < 2 * SWA_HD, linv[1],
                                   jnp.where(lane < 3 * SWA_HD, linv[2], linv[3])))
        o_ref[r0:r0 + tq, kh * SWA_GW:(kh + 1) * SWA_GW] = (acc * norm).astype(BF16)

    return [prologue] + [functools.partial(head, kh) for kh in range(SWA_KVH)]


def _swa_kernel(sink_ref, qkv_ref, kprev_ref, vprev_ref, o_ref, kblk, vblk, bias_scr, *, tq, prev_always_valid):
    t = pl.program_id(1)

    @pl.when(jnp.logical_and(pl.program_id(0) == 0, t == 0))
    def _():
        kblk[...] = jnp.zeros(kblk.shape, BF16)
        vblk[...] = jnp.zeros(vblk.shape, BF16)

    _attend(qkv_ref, 0, tq, lambda: kprev_ref[...], lambda: vprev_ref[...],
            None if prev_always_valid else t > 0, sink_ref, kblk, vblk, bias_scr, o_ref)


def _swa_attention(qkv, k_prev, v_prev, sinks, batch, t_len, prev_from_qkv):
    tq = min(t_len, WINDOW)
    nt = t_len // tq
    if prev_from_qkv:
        assert tq == WINDOW
        kspec = pl.BlockSpec((WINDOW, SWA_DKV),
                             lambda b, t: (jnp.maximum(b * nt + t - 1, 0), SWA_DQ // SWA_DKV))
        vspec = pl.BlockSpec((WINDOW, SWA_DKV),
                             lambda b, t: (jnp.maximum(b * nt + t - 1, 0), SWA_DQ // SWA_DKV + 1))
        k_prev = v_prev = qkv
    else:
        kspec = pl.BlockSpec((WINDOW, SWA_DKV), lambda b, t: (b, 0))
        vspec = pl.BlockSpec((WINDOW, SWA_DKV), lambda b, t: (b, 0))
    return pl.pallas_call(
        functools.partial(_swa_kernel, tq=tq, prev_always_valid=not prev_from_qkv),
        grid=(batch, nt),
        in_specs=[
            pl.BlockSpec(memory_space=pltpu.SMEM),
            pl.BlockSpec((tq, SWA_COLS), lambda b, t: (b * nt + t, 0)),
            kspec,
            vspec,
        ],
        out_specs=pl.BlockSpec((tq, SWA_DQ), lambda b, t: (b * nt + t, 0)),
        out_shape=jax.ShapeDtypeStruct((batch * t_len, SWA_DQ), BF16),
        scratch_shapes=[
            pltpu.VMEM((SWA_KVH, SWA_GW, SWA_GROUP * SWA_NK), BF16),
            pltpu.VMEM((SWA_KVH, SWA_GROUP * SWA_NK, SWA_GW), BF16),
            pltpu.VMEM((tq, SWA_NK), F32),
        ],
        compiler_params=_params(("arbitrary", "arbitrary")),
        name="swa_attention",
    )(sinks, qkv, k_prev, v_prev)


FF_CHUNK = 1024
LAYER_FF_CHUNK = 512


def _post_rows(o, x, g_ref, wo_ref, bo_ref, wu_ref, wd_ref):
    a = _dot(o, wo_ref[...]) + bo_ref[...]
    x1 = x + _rms(a, g_ref[1:2, :])
    h = _rms(x1, g_ref[2:3, :]).astype(BF16)
    acc = None
    for j in range(D_FF // FF_CHUNK):
        u = _dot(h, wu_ref[:, j * FF_CHUNK:(j + 1) * FF_CHUNK])
        u = jnp.maximum(u, 0.0)
        u = (u * u).astype(BF16)
        part = _dot(u, wd_ref[j * FF_CHUNK:(j + 1) * FF_CHUNK, :])
        acc = part if acc is None else acc + part
    return x1 + _rms(acc, g_ref[3:4, :])


def _post_kernel(*refs, with_next):
    if with_next:
        o_ref, x_ref, g_ref, wo_ref, bo_ref, wu_ref, wd_ref, gn_ref, wn_ref, bn_ref, y_ref, p_ref = refs
    else:
        o_ref, x_ref, g_ref, wo_ref, bo_ref, wu_ref, wd_ref, y_ref = refs
    y = _post_rows(o_ref[...], x_ref[...], g_ref, wo_ref, bo_ref, wu_ref, wd_ref)
    y_ref[...] = y
    if with_next:
        hn = _rms(y, gn_ref[0:1, :]).astype(BF16)
        p_ref[...] = _dot(hn, wn_ref[...]) + bn_ref[...]


def _swa_layer_kernel(sink_ref, qkv_ref, kprev_ref, vprev_ref, x_ref, g_ref, wo_ref, bo_ref, wu_ref, wd_ref,
                      y_ref, kblk, vblk, bias_scr, o_scr, a_scr, *, n_tiles, tile, blocks_per_seq):
    s = pl.program_id(0)

    @pl.when(s == 0)
    def _():
        kblk[...] = jnp.zeros(kblk.shape, BF16)
        vblk[...] = jnp.zeros(vblk.shape, BF16)
        o_scr[...] = jnp.zeros(o_scr.shape, BF16)

    a_tile = jnp.minimum(s, n_tiles - 1)
    n_sub = tile // WINDOW

    pieces = []
    for j in range(n_sub):
        r0 = j * WINDOW
        if j == 0:
            k_prev, v_prev = (lambda: kprev_ref[...]), (lambda: vprev_ref[...])
            prev_valid = lax.rem(a_tile * n_sub, blocks_per_seq) != 0
        else:
            k_prev = functools.partial(lambda r: qkv_ref[r - WINDOW:r, SWA_DQ:SWA_DQ + SWA_DKV], r0)
            v_prev = functools.partial(lambda r: qkv_ref[r - WINDOW:r, SWA_DQ + SWA_DKV:], r0)
            prev_valid = None
        pieces += _attend_pieces(qkv_ref, r0, WINDOW, k_prev, v_prev, prev_valid,
                                 sink_ref, kblk, vblk, bias_scr, a_scr)

    n_ff = D_FF // LAYER_FF_CHUNK
    slots = n_ff
    per_slot = -(-len(pieces) // slots)

    def run_pieces():
        for _ in range(per_slot):
            if pieces:
                pieces.pop(0)()

    a = _dot(o_scr[...], wo_ref[...]) + bo_ref[...]
    x1 = x_ref[...] + _rms(a, g_ref[1:2, :])
    h = _rms(x1, g_ref[2:3, :]).astype(BF16)
    run_pieces()
    acc = None
    for j in range(n_ff):
        u = _dot(h, wu_ref[:, j * LAYER_FF_CHUNK:(j + 1) * LAYER_FF_CHUNK])
        u = jnp.maximum(u, 0.0)
        u = (u * u).astype(BF16)
        part = _dot(u, wd_ref[j * LAYER_FF_CHUNK:(j + 1) * LAYER_FF_CHUNK, :])
        acc = part if acc is None else acc + part
        if j + 1 < n_ff:
            run_pieces()
    assert not pieces
    y_ref[...] = x1 + _rms(acc, g_ref[3:4, :])
    o_scr[...] = a_scr[...]


def _swa_layer(qkv, x, sinks, g4, wo, bo, wu, wd, t_len, tile):
    n, d = x.shape
    assert t_len % tile == 0 and tile % WINDOW == 0
    n_tiles = n // tile
    per = tile // WINDOW
    a_idx = lambda s: jnp.minimum(s, n_tiles - 1)
    p_idx = lambda s: jnp.maximum(s - 1, 0)
    return pl.pallas_call(
        functools.partial(_swa_layer_kernel, n_tiles=n_tiles, tile=tile, blocks_per_seq=t_len // WINDOW),
        grid=(n_tiles + 1,),
        in_specs=[
            pl.BlockSpec(memory_space=pltpu.SMEM),
            pl.BlockSpec((tile, SWA_COLS), lambda s: (a_idx(s), 0)),
            pl.BlockSpec((WINDOW, SWA_DKV), lambda s: (jnp.maximum(a_idx(s) * per - 1, 0), SWA_DQ // SWA_DKV)),
            pl.BlockSpec((WINDOW, SWA_DKV), lambda s: (jnp.maximum(a_idx(s) * per - 1, 0), SWA_DQ // SWA_DKV + 1)),
            pl.BlockSpec((tile, d), lambda s: (p_idx(s), 0)),
            _resident((4, d)),
            _resident((d, d)),
            _resident((1, d)),
            _resident((d, D_FF)),
            _resident((D_FF, d)),
        ],
        out_specs=pl.BlockSpec((tile, d), lambda s: (p_idx(s), 0)),
        out_shape=jax.ShapeDtypeStruct((n, d), F32),
        scratch_shapes=[
            pltpu.VMEM((SWA_KVH, SWA_GW, SWA_GROUP * SWA_NK), BF16),
            pltpu.VMEM((SWA_KVH, SWA_GROUP * SWA_NK, SWA_GW), BF16),
            pltpu.VMEM((WINDOW, SWA_NK), F32),
            pltpu.VMEM((tile, SWA_DQ), BF16),
            pltpu.VMEM((tile, SWA_DQ), BF16),
        ],
        compiler_params=_params(("arbitrary",)),
        name="swa_layer",
    )(sinks, qkv, qkv, qkv, x, g4, wo, bo, wu, wd)


def _post_block(o, x, g4, wo, bo, wu, wd, tm, nxt=None):
    n, d = x.shape
    rows = lambda i: (i, 0)
    in_specs = [
        pl.BlockSpec((tm, d), rows),
        pl.BlockSpec((tm, d), rows),
        _resident((4, d)),
        _resident((d, d)),
        _resident((1, d)),
        _resident((d, D_FF)),
        _resident((D_FF, d)),
    ]
    args = [o, x, g4, wo, bo, wu, wd]
    out_specs = [pl.BlockSpec((tm, d), rows)]
    out_shape = [jax.ShapeDtypeStruct((n, d), F32)]
    if nxt is not None:
        gn, wn, bn = nxt
        cols = wn.shape[1]
        in_specs += [_resident((4, d)), _resident((d, cols)), _resident((1, cols))]
        args += [gn, wn, bn]
        out_specs.append(pl.BlockSpec((tm, cols), rows))
        out_shape.append(jax.ShapeDtypeStruct((n, cols), F32))
    outs = pl.pallas_call(
        functools.partial(_post_kernel, with_next=nxt is not None),
        grid=(n // tm,),
        in_specs=in_specs,
        out_specs=out_specs,
        out_shape=out_shape,
        compiler_params=_params(("arbitrary",)),
        name="post_block",
    )(*args)
    return outs if nxt is not None else outs[0]


def _trunk(x, gla_s0, swa_k0, swa_v0, w):
    batch, t_len, d = x.shape
    n = batch * t_len
    tm = min(ROW_TILE, n)
    assert n % tm == 0 and t_len % CHUNK == 0
    xf = x.reshape(n, d)
    og, s_fin = _gla_layer(xf, w["norm_g"][0], w["gla_w_ext"], w["gla_wb"], w["gla_bg"], w["gla_gon"],
                           gla_s0, batch, t_len)
    x1, qkv = _post_block(og, xf, w["norm_g"][0], w["gla_wo"], w["zero_bias"], w["wu"][0], w["wd"][0], tm,
                          nxt=(w["norm_g"][1], w["swa_wi"], w["swa_bi"]))
    if swa_k0 is None and t_len % SWA_LAYER_TILE == 0:
        y = _swa_layer(qkv, x1, w["sinks"], w["norm_g"][1], w["swa_wo"], w["swa_bo"], w["wu"][1], w["wd"][1],
                       t_len, SWA_LAYER_TILE)
    else:
        if swa_k0 is None:
            oa = _swa_attention(qkv, None, None, w["sinks"], batch, t_len, True)
        else:
            kp = swa_k0.reshape(batch * WINDOW, SWA_DKV)
            vp = swa_v0.reshape(batch * WINDOW, SWA_DKV)
            oa = _swa_attention(qkv, kp, vp, w["sinks"], batch, t_len, False)
        y = _post_block(oa, x1, w["norm_g"][1], w["swa_wo"], w["swa_bo"], w["wu"][1], w["wd"][1], tm)
    keep = min(WINDOW, t_len) if swa_k0 is None else swa_k0.shape[1]
    new = min(keep, t_len)
    tail = qkv.reshape(batch, t_len, SWA_COLS)[:, t_len - new:, :]
    k_new = tail[:, :, SWA_DQ:SWA_DQ + SWA_DKV].reshape(batch, new, SWA_KVH, SWA_HD)
    v_new = tail[:, :, SWA_DQ + SWA_DKV:].reshape(batch, new, SWA_KVH, SWA_HD)
    if swa_k0 is None:
        k_cache, v_cache = k_new, v_new
    else:
        k_cache = jnp.concatenate([swa_k0[:, new:], k_new], axis=1)
        v_cache = jnp.concatenate([swa_v0[:, new:], v_new], axis=1)
    return y.reshape(batch, t_len, d), s_fin[None], k_cache[None], v_cache[None]


def kernel(x_prompt, x_sample, state_gla, cache_swa_k, cache_swa_v, gla_w_in, gla_w_gate_a, gla_w_gate_b, gla_b_gate, gla_g_onorm, gla_w_out, swa_w_in, swa_b_in, swa_sinks, swa_w_out, swa_b_out, norm_g, mlp_w_up, mlp_w_down):
    d = D_MODEL
    pad_a = jnp.zeros((d, GLA_GATE_PAD - GLA_GATE_RANK), F32)
    pad_b = jnp.zeros((GLA_GATE_PAD - GLA_GATE_RANK, GLA_DQK), F32)
    w = {
        "norm_g": norm_g,
        "gla_w_ext": jnp.concatenate([gla_w_in[0], gla_w_gate_a[0], pad_a], axis=1).astype(BF16),
        "gla_wb": jnp.concatenate([gla_w_gate_b[0], pad_b], axis=0).astype(BF16),
        "gla_bg": gla_b_gate[0].reshape(1, GLA_DQK),
        "gla_gon": gla_g_onorm[0].reshape(1, GLA_DVT),
        "gla_wo": gla_w_out[0].astype(BF16),
        "zero_bias": jnp.zeros((1, d), F32),
        "swa_wi": swa_w_in[0].astype(BF16),
        "swa_bi": swa_b_in[0].reshape(1, SWA_COLS),
        "sinks": swa_sinks[0],
        "swa_wo": swa_w_out[0].astype(BF16),
        "swa_bo": swa_b_out[0].reshape(1, d),
        "wu": mlp_w_up.astype(BF16),
        "wd": mlp_w_down.astype(BF16),
    }
    y_p, s_p, k_p, v_p = _trunk(x_prompt, None, None, None, w)
    y_s, s_s, k_s, v_s = _trunk(x_sample, state_gla[0], cache_swa_k[0], cache_swa_v[0], w)
    return (y_p, y_s, s_p, s_s, k_p, v_p, k_s, v_s)
```

```python
import functools
import math

import numpy as np
import jax
import jax.numpy as jnp
from jax import lax
from jax.experimental import pallas as pl
from jax.experimental.pallas import tpu as pltpu

F32 = jnp.float32
BF16 = jnp.bfloat16

D_MODEL = 1024
D_FF = 4 * D_MODEL
EPS = 1e-6
CHUNK = 64
LOG2E = math.log2(math.e)
GLA_HEADS = 4
GLA_DK = 128
GLA_DV = 256
GLA_DQK = GLA_HEADS * GLA_DK
GLA_DVT = GLA_HEADS * GLA_DV
GLA_GATE_RANK = 16
GLA_GATE_NORM = 16.0
GLA_GATE_PAD = 128
GLA_PROJ_COLS = 2 * GLA_DQK + 2 * GLA_DVT + GLA_GATE_PAD
GLA_MAX_CHUNK = 256
GLA_SAFE_DECAY = 60.0
SWA_HD = 64
SWA_QH = 16
SWA_KVH = 4
SWA_GROUP = 4
SWA_DQ = SWA_QH * SWA_HD
SWA_DKV = SWA_KVH * SWA_HD
SWA_COLS = SWA_DQ + 2 * SWA_DKV
SWA_GW = SWA_GROUP * SWA_HD
WINDOW = 128
SWA_NK = 2 * WINDOW
ROW_TILE = 512
SWA_LAYER_TILE = 512
VMEM_LIMIT_BYTES = 60 * 1024 * 1024


def _dot(a, b):
    return jnp.dot(a, b, preferred_element_type=F32)


def _dot_nt(a, b):
    return lax.dot_general(a, b, (((1,), (1,)), ((), ())), preferred_element_type=F32)


def _dot_tn(a, b):
    return lax.dot_general(a, b, (((0,), (0,)), ((), ())), preferred_element_type=F32)


def _rms(x, g):
    return x * lax.rsqrt(jnp.mean(x * x, axis=-1, keepdims=True) + EPS) * g


def _split_bf16(x):
    hi = x.astype(BF16)
    lo = (x - hi.astype(F32)).astype(BF16)
    return hi, lo


def _params(semantics):
    return pltpu.CompilerParams(dimension_semantics=semantics, vmem_limit_bytes=VMEM_LIMIT_BYTES)


def _resident(shape):
    zeros = (0,) * len(shape)
    return pl.BlockSpec(shape, lambda *_: zeros, pipeline_mode=pl.Buffered(1))


def _gla_level_matrices(c):
    levels = []
    s = c
    while s >= 2:
        levels.append(s)
        s //= 2
    t = np.arange(c)[None, :]
    r = np.arange(c)[:, None]
    mats = []
    for s in levels:
        mid = (r // s) * s + s // 2
        mq = (r >= mid) & (t >= mid) & (t <= r)
        mk = (r < mid) & (t > r) & (t <= mid - 1)
        mats.append(mq)
        mats.append(mk)
    return np.stack(mats).astype(np.float32), levels


def _gla_kernel(*refs, c, levels, has_s0):
    if has_s0:
        (x_ref, g_ref, w_ref, wb_ref, bg_ref, gon_ref, tri_ref, lev_ref, s0_ref,
         og_ref, s_ref, a_scr, sprev) = refs
    else:
        (x_ref, g_ref, w_ref, wb_ref, bg_ref, gon_ref, tri_ref, lev_ref,
         og_ref, s_ref, a_scr, sprev) = refs
        s0_ref = None

    @pl.when(pl.program_id(1) == 0)
    def _():
        if has_s0:
            s_ref[...] = s0_ref[...]
        else:
            s_ref[...] = jnp.zeros(s_ref.shape, F32)

    scale = GLA_DK ** -0.5
    row = lax.broadcasted_iota(jnp.int32, (c, c), 0)
    col = lax.broadcasted_iota(jnp.int32, (c, c), 1)
    nb = min(c, GLA_DK)

    def step(robust):
        h = _rms(x_ref[...], g_ref[0:1, :]).astype(BF16)
        qk = _dot(h, w_ref[:, :2 * GLA_DQK])
        v_all = _dot(h, w_ref[:, 2 * GLA_DQK:2 * GLA_DQK + GLA_DVT]).astype(BF16)
        r_all = _dot(h, w_ref[:, 2 * GLA_DQK + GLA_DVT:2 * GLA_DQK + 2 * GLA_DVT])
        ga = _dot(h, w_ref[:, 2 * GLA_DQK + 2 * GLA_DVT:]).astype(BF16)
        gp = _dot(ga, wb_ref[...]) + bg_ref[...]
        lf = (jnp.minimum(gp, 0.0) - jnp.log(1.0 + jnp.exp(-jnp.abs(gp)))) * (1.0 / GLA_GATE_NORM)
        lf_hi, lf_lo = _split_bf16(lf)
        tri = tri_ref[...]
        b = _dot(tri, lf_hi) + _dot(tri, lf_lo)
        b_last = b[c - 1:c, :]

        if robust:
            for hd in range(GLA_HEADS):
                cs = slice(hd * GLA_DK, (hd + 1) * GLA_DK)
                ks = slice(GLA_DQK + hd * GLA_DK, GLA_DQK + (hd + 1) * GLA_DK)
                a_scr[hd] = jnp.where(row == col,
                                      _dot_nt((qk[:, cs] * scale).astype(BF16), qk[:, ks].astype(BF16)), 0.0)
            for li, s in enumerate(levels):
                mq = lev_ref[2 * li]
                mk = lev_ref[2 * li + 1]
                dq = _dot(mq, lf_hi) + _dot(mq, lf_lo)
                dk = _dot(mk, lf_hi) + _dot(mk, lf_lo)
                half = s // 2
                sh = s.bit_length() - 1
                same = lax.shift_right_logical(row, sh) == lax.shift_right_logical(col, sh)
                mask = same & ((row & (s - 1)) >= half) & ((col & (s - 1)) < half)
                for hd in range(GLA_HEADS):
                    cs = slice(hd * GLA_DK, (hd + 1) * GLA_DK)
                    ks = slice(GLA_DQK + hd * GLA_DK, GLA_DQK + (hd + 1) * GLA_DK)
                    qq = (qk[:, cs] * scale * jnp.exp(dq[:, cs])).astype(BF16)
                    kk = (qk[:, ks] * jnp.exp(dk[:, cs])).astype(BF16)
                    a_scr[hd] = a_scr[hd] + jnp.where(mask, _dot_nt(qq, kk), 0.0)

        heads = range(GLA_HEADS)
        cs = [slice(hd * GLA_DK, (hd + 1) * GLA_DK) for hd in heads]
        ks = [slice(GLA_DQK + hd * GLA_DK, GLA_DQK + (hd + 1) * GLA_DK) for hd in heads]
        vs = [slice(hd * GLA_DV, (hd + 1) * GLA_DV) for hd in heads]
        qe = [(qk[:, cs[hd]] * scale * jnp.exp(b[:, cs[hd]])).astype(BF16) for hd in heads]
        if robust:
            s_old = [sprev[hd] for hd in heads]
            a = [a_scr[hd].astype(BF16) for hd in heads]
        else:
            s_old = [s_ref[0, hd] for hd in heads]
            for hd in heads:
                sprev[hd] = s_old[hd]
            ke = [(qk[:, ks[hd]] * jnp.exp(-b[:, cs[hd]])).astype(BF16) for hd in heads]
            a = [jnp.where(row >= col, _dot_nt(qe[hd], ke[hd]), 0.0).astype(BF16) for hd in heads]
        o = [_dot(qe[hd], s_old[hd].astype(BF16)) + _dot(a[hd], v_all[:, vs[hd]]) for hd in heads]
        if not robust:
            kd = [(qk[:, ks[hd]] * jnp.exp(b_last[:, cs[hd]] - b[:, cs[hd]])).astype(BF16) for hd in heads]
            for hd in heads:
                dec = jnp.exp(jnp.transpose(b[c - nb:c, cs[hd]])[:, nb - 1:nb])
                s_ref[0, hd] = s_old[hd] * dec + _dot_tn(kd[hd], v_all[:, vs[hd]])
        for hd in heads:
            on = o[hd] * lax.rsqrt(jnp.mean(o[hd] * o[hd], axis=-1, keepdims=True) + EPS) * gon_ref[:, vs[hd]]
            hr = 0.5 * r_all[:, vs[hd]]
            og_ref[:, vs[hd]] = (on * (hr + hr * jnp.tanh(hr))).astype(BF16)
        return jnp.min(b)

    b_min = step(False)

    @pl.when(b_min < -GLA_SAFE_DECAY)
    def _():
        step(True)


def _gla_layer(x, g4, w_ext, wb, bg, gon, s0, batch, t_len):
    d = x.shape[1]
    c = min(t_len, GLA_MAX_CHUNK)
    nt = t_len // c
    lev_np, levels = _gla_level_matrices(c)
    lev = jnp.asarray(lev_np, BF16)
    tri = jnp.asarray(np.tril(np.ones((c, c), np.float32)), BF16)
    has_s0 = s0 is not None
    in_specs = [
        pl.BlockSpec((c, d), lambda b, t: (b * nt + t, 0)),
        _resident((4, d)),
        _resident((d, GLA_PROJ_COLS)),
        _resident((GLA_GATE_PAD, GLA_DQK)),
        _resident((1, GLA_DQK)),
        _resident((1, GLA_DVT)),
        _resident((c, c)),
        _resident((2 * len(levels), c, c)),
    ]
    args = [x, g4, w_ext, wb, bg, gon, tri, lev]
    if has_s0:
        in_specs.append(pl.BlockSpec((1, GLA_HEADS, GLA_DK, GLA_DV), lambda b, t: (b, 0, 0, 0)))
        args.append(s0)
    og, s_fin = pl.pallas_call(
        functools.partial(_gla_kernel, c=c, levels=tuple(levels), has_s0=has_s0),
        grid=(batch, nt),
        in_specs=in_specs,
        out_specs=[
            pl.BlockSpec((c, GLA_DVT), lambda b, t: (b * nt + t, 0)),
            pl.BlockSpec((1, GLA_HEADS, GLA_DK, GLA_DV), lambda b, t: (b, 0, 0, 0)),
        ],
        out_shape=[
            jax.ShapeDtypeStruct((batch * t_len, GLA_DVT), BF16),
            jax.ShapeDtypeStruct((batch, GLA_HEADS, GLA_DK, GLA_DV), F32),
        ],
        scratch_shapes=[
            pltpu.VMEM((GLA_HEADS, c, c), F32),
            pltpu.VMEM((GLA_HEADS, GLA_DK, GLA_DV), F32),
        ],
        compiler_params=_params(("arbitrary", "arbitrary")),
        name="gla_layer",
    )(*args)
    return og, s_fin


def _attend(qkv_ref, r0, tq, k_prev, v_prev, prev_valid, sink_ref, kblk, vblk, bias_scr, o_ref):
    for piece in _attend_pieces(qkv_ref, r0, tq, k_prev, v_prev, prev_valid, sink_ref, kblk, vblk, bias_scr, o_ref):
        piece()


def _attend_pieces(qkv_ref, r0, tq, k_prev, v_prev, prev_valid, sink_ref, kblk, vblk, bias_scr, o_ref,
                   own_slot=None):
    def prologue():
        if own_slot is None:
            pad = SWA_NK - WINDOW - tq
            k_parts = [k_prev(), qkv_ref[r0:r0 + tq, SWA_DQ:SWA_DQ + SWA_DKV]]
            v_parts = [v_prev(), qkv_ref[r0:r0 + tq, SWA_DQ + SWA_DKV:]]
            if pad:
                k_parts.append(jnp.zeros((pad, SWA_DKV), F32))
                v_parts.append(jnp.zeros((pad, SWA_DKV), F32))
            kt = jnp.concatenate(k_parts, axis=0).T.astype(BF16)
            vc = jnp.concatenate(v_parts, axis=0).astype(BF16)
            c0, nc = 0, SWA_NK
        else:
            assert tq == WINDOW
            kt = qkv_ref[r0:r0 + tq, SWA_DQ:SWA_DQ + SWA_DKV].T.astype(BF16)
            vc = qkv_ref[r0:r0 + tq, SWA_DQ + SWA_DKV:].astype(BF16)
            c0, nc = own_slot * WINDOW, WINDOW
        for kh in range(SWA_KVH):
            for g in range(SWA_GROUP):
                kblk[kh, g * SWA_HD:(g + 1) * SWA_HD, g * SWA_NK + c0:g * SWA_NK + c0 + nc] = \
                    kt[kh * SWA_HD:(kh + 1) * SWA_HD, :]
                vblk[kh, g * SWA_NK + c0:g * SWA_NK + c0 + nc, g * SWA_HD:(g + 1) * SWA_HD] = \
                    vc[:, kh * SWA_HD:(kh + 1) * SWA_HD]
        qrow = lax.broadcasted_iota(jnp.int32, (tq, SWA_NK), 0)
        kcol = lax.broadcasted_iota(jnp.int32, (tq, SWA_NK), 1)
        if own_slot == 0:
            kcol = jnp.where(kcol < WINDOW, kcol + WINDOW, kcol - WINDOW)
        lo = lax.shift_right_logical(qrow, 6) * CHUNK
        mask = (kcol >= lo) & (kcol < lo + WINDOW + CHUNK) & (kcol < WINDOW + tq)
        if prev_valid is not None:
            mask = mask & jnp.logical_or(kcol >= WINDOW, prev_valid)
        bias_scr[...] = jnp.where(mask, 0.0, -jnp.inf)

    def head(kh):
        lane = lax.broadcasted_iota(jnp.int32, (tq, SWA_GW), 1)
        q4 = (qkv_ref[r0:r0 + tq, kh * SWA_GW:(kh + 1) * SWA_GW] * (SWA_HD ** -0.5 * LOG2E)).astype(BF16)
        acc = None
        linv = []
        for g in range(SWA_GROUP):
            sg = _dot(q4, kblk[kh, :, g * SWA_NK:(g + 1) * SWA_NK]) + bias_scr[...]
            sink = sink_ref[kh * SWA_GROUP + g] * LOG2E
            m = jnp.maximum(jnp.max(sg, axis=-1, keepdims=True), sink)
            p = jnp.exp2(sg - m)
            linv.append(1.0 / (jnp.sum(p, axis=-1, keepdims=True) + jnp.exp2(sink - m)))
            part = _dot(p.astype(BF16), vblk[kh, g * SWA_NK:(g + 1) * SWA_NK, :])
            acc = part if acc is None else acc + part
        norm = jnp.where(lane < SWA_HD, linv[0],
                         jnp.where(lane < 2 * SWA_HD, linv[1],
                                   jnp.where(lane < 3 * SWA_HD, linv[2], linv[3])))
        o_ref[r0:r0 + tq, kh * SWA_GW:(kh + 1) * SWA_GW] = (acc * norm).astype(BF16)

    return [prologue] + [functools.partial(head, kh) for kh in range(SWA_KVH)]


def _swa_kernel(sink_ref, qkv_ref, kprev_ref, vprev_ref, o_ref, kblk, vblk, bias_scr, *, tq, prev_always_valid):
    t = pl.program_id(1)

    @pl.when(jnp.logical_and(pl.program_id(0) == 0, t == 0))
    def _():
        kblk[...] = jnp.zeros(kblk.shape, BF16)
        vblk[...] = jnp.zeros(vblk.shape, BF16)

    _attend(qkv_ref, 0, tq, lambda: kprev_ref[...], lambda: vprev_ref[...],
            None if prev_always_valid else t > 0, sink_ref, kblk, vblk, bias_scr, o_ref)


def _swa_attention(qkv, k_prev, v_prev, sinks, batch, t_len, prev_from_qkv):
    tq = min(t_len, WINDOW)
    nt = t_len // tq
    if prev_from_qkv:
        assert tq == WINDOW
        kspec = pl.BlockSpec((WINDOW, SWA_DKV),
                             lambda b, t: (jnp.maximum(b * nt + t - 1, 0), SWA_DQ // SWA_DKV))
        vspec = pl.BlockSpec((WINDOW, SWA_DKV),
                             lambda b, t: (jnp.maximum(b * nt + t - 1, 0), SWA_DQ // SWA_DKV + 1))
        k_prev = v_prev = qkv
    else:
        kspec = pl.BlockSpec((WINDOW, SWA_DKV), lambda b, t: (b, 0))
        vspec = pl.BlockSpec((WINDOW, SWA_DKV), lambda b, t: (b, 0))
    return pl.pallas_call(
        functools.partial(_swa_kernel, tq=tq, prev_always_valid=not prev_from_qkv),
        grid=(batch, nt),
        in_specs=[
            pl.BlockSpec(memory_space=pltpu.SMEM),
            pl.BlockSpec((tq, SWA_COLS), lambda b, t: (b * nt + t, 0)),
            kspec,
            vspec,
        ],
        out_specs=pl.BlockSpec((tq, SWA_DQ), lambda b, t: (b * nt + t, 0)),
        out_shape=jax.ShapeDtypeStruct((batch * t_len, SWA_DQ), BF16),
        scratch_shapes=[
            pltpu.VMEM((SWA_KVH, SWA_GW, SWA_GROUP * SWA_NK), BF16),
            pltpu.VMEM((SWA_KVH, SWA_GROUP * SWA_NK, SWA_GW), BF16),
            pltpu.VMEM((tq, SWA_NK), F32),
        ],
        compiler_params=_params(("arbitrary", "arbitrary")),
        name="swa_attention",
    )(sinks, qkv, k_prev, v_prev)


FF_CHUNK = 1024
POST_ROW_GROUPS = 2
LAYER_FF_CHUNK = 512


def _post_rows(o, x, g_ref, wo_ref, bo_ref, wu_ref, wd_ref):
    m = o.shape[0]
    ng = POST_ROW_GROUPS if m % (POST_ROW_GROUPS * 128) == 0 else 1
    rows = [slice(i * (m // ng), (i + 1) * (m // ng)) for i in range(ng)]
    a = [_dot(o[r], wo_ref[...]) + bo_ref[...] for r in rows]
    x1 = [x[r] + _rms(a[i], g_ref[1:2, :]) for i, r in enumerate(rows)]
    h = [_rms(x1[i], g_ref[2:3, :]).astype(BF16) for i in range(ng)]
    acc = [None] * ng
    for j in range(D_FF // FF_CHUNK):
        for i in range(ng):
            u = _dot(h[i], wu_ref[:, j * FF_CHUNK:(j + 1) * FF_CHUNK])
            u = jnp.maximum(u, 0.0)
            u = (u * u).astype(BF16)
            part = _dot(u, wd_ref[j * FF_CHUNK:(j + 1) * FF_CHUNK, :])
            acc[i] = part if acc[i] is None else acc[i] + part
    y = [x1[i] + _rms(acc[i], g_ref[3:4, :]) for i in range(ng)]
    return y[0] if ng == 1 else jnp.concatenate(y, axis=0)


def _post_kernel(*refs, with_next):
    if with_next:
        o_ref, x_ref, g_ref, wo_ref, bo_ref, wu_ref, wd_ref, gn_ref, wn_ref, bn_ref, y_ref, p_ref = refs
    else:
        o_ref, x_ref, g_ref, wo_ref, bo_ref, wu_ref, wd_ref, y_ref = refs
    y = _post_rows(o_ref[...], x_ref[...], g_ref, wo_ref, bo_ref, wu_ref, wd_ref)
    y_ref[...] = y
    if with_next:
        hn = _rms(y, gn_ref[0:1, :]).astype(BF16)
        p_ref[...] = _dot(hn, wn_ref[...]) + bn_ref[...]


def _swa_layer_kernel(sink_ref, qkv_ref, kprev_ref, vprev_ref, x_ref, g_ref, wo_ref, bo_ref, wu_ref, wd_ref,
                      y_ref, kblk, vblk, bias_scr, o_scr, a_scr, *, n_tiles, tile, blocks_per_seq):
    s = pl.program_id(0)

    @pl.when(s == 0)
    def _():
        kblk[...] = jnp.zeros(kblk.shape, BF16)
        vblk[...] = jnp.zeros(vblk.shape, BF16)
        o_scr[...] = jnp.zeros(o_scr.shape, BF16)

    a_tile = jnp.minimum(s, n_tiles - 1)
    n_sub = tile // WINDOW

    pieces = []
    for j in range(n_sub):
        r0 = j * WINDOW
        if j == 0:
            k_prev, v_prev = (lambda: kprev_ref[...]), (lambda: vprev_ref[...])
            prev_valid = lax.rem(a_tile * n_sub, blocks_per_seq) != 0
        else:
            k_prev = functools.partial(lambda r: qkv_ref[r - WINDOW:r, SWA_DQ:SWA_DQ + SWA_DKV], r0)
            v_prev = functools.partial(lambda r: qkv_ref[r - WINDOW:r, SWA_DQ + SWA_DKV:], r0)
            prev_valid = None
        pieces += _attend_pieces(qkv_ref, r0, WINDOW, k_prev, v_prev, prev_valid,
                                 sink_ref, kblk, vblk, bias_scr, a_scr)

    n_ff = D_FF // LAYER_FF_CHUNK
    slots = n_ff
    per_slot = -(-len(pieces) // slots)

    def run_pieces():
        for _ in range(per_slot):
            if pieces:
                pieces.pop(0)()

    a = _dot(o_scr[...], wo_ref[...]) + bo_ref[...]
    x1 = x_ref[...] + _rms(a, g_ref[1:2, :])
    h = _rms(x1, g_ref[2:3, :]).astype(BF16)
    run_pieces()
    acc = None
    for j in range(n_ff):
        u = _dot(h, wu_ref[:, j * LAYER_FF_CHUNK:(j + 1) * LAYER_FF_CHUNK])
        u = jnp.maximum(u, 0.0)
        u = (u * u).astype(BF16)
        part = _dot(u, wd_ref[j * LAYER_FF_CHUNK:(j + 1) * LAYER_FF_CHUNK, :])
        acc = part if acc is None else acc + part
        if j + 1 < n_ff:
            run_pieces()
    assert not pieces
    y_ref[...] = x1 + _rms(acc, g_ref[3:4, :])
    o_scr[...] = a_scr[...]


def _swa_layer(qkv, x, sinks, g4, wo, bo, wu, wd, t_len, tile):
    n, d = x.shape
    assert t_len % tile == 0 and tile % WINDOW == 0
    n_tiles = n // tile
    per = tile // WINDOW
    a_idx = lambda s: jnp.minimum(s, n_tiles - 1)
    p_idx = lambda s: jnp.maximum(s - 1, 0)
    return pl.pallas_call(
        functools.partial(_swa_layer_kernel, n_tiles=n_tiles, tile=tile, blocks_per_seq=t_len // WINDOW),
        grid=(n_tiles + 1,),
        in_specs=[
            pl.BlockSpec(memory_space=pltpu.SMEM),
            pl.BlockSpec((tile, SWA_COLS), lambda s: (a_idx(s), 0)),
            pl.BlockSpec((WINDOW, SWA_DKV), lambda s: (jnp.maximum(a_idx(s) * per - 1, 0), SWA_DQ // SWA_DKV)),
            pl.BlockSpec((WINDOW, SWA_DKV), lambda s: (jnp.maximum(a_idx(s) * per - 1, 0), SWA_DQ // SWA_DKV + 1)),
            pl.BlockSpec((tile, d), lambda s: (p_idx(s), 0)),
            _resident((4, d)),
            _resident((d, d)),
            _resident((1, d)),
            _resident((d, D_FF)),
            _resident((D_FF, d)),
        ],
        out_specs=pl.BlockSpec((tile, d), lambda s: (p_idx(s), 0)),
        out_shape=jax.ShapeDtypeStruct((n, d), F32),
        scratch_shapes=[
            pltpu.VMEM((SWA_KVH, SWA_GW, SWA_GROUP * SWA_NK), BF16),
            pltpu.VMEM((SWA_KVH, SWA_GROUP * SWA_NK, SWA_GW), BF16),
            pltpu.VMEM((WINDOW, SWA_NK), F32),
            pltpu.VMEM((tile, SWA_DQ), BF16),
            pltpu.VMEM((tile, SWA_DQ), BF16),
        ],
        compiler_params=_params(("arbitrary",)),
        name="swa_layer",
    )(sinks, qkv, qkv, qkv, x, g4, wo, bo, wu, wd)


def _post_block(o, x, g4, wo, bo, wu, wd, tm, nxt=None):
    n, d = x.shape
    rows = lambda i: (i, 0)
    in_specs = [
        pl.BlockSpec((tm, d), rows),
        pl.BlockSpec((tm, d), rows),
        _resident((4, d)),
        _resident((d, d)),
        _resident((1, d)),
        _resident((d, D_FF)),
        _resident((D_FF, d)),
    ]
    args = [o, x, g4, wo, bo, wu, wd]
    out_specs = [pl.BlockSpec((tm, d), rows)]
    out_shape = [jax.ShapeDtypeStruct((n, d), F32)]
    if nxt is not None:
        gn, wn, bn = nxt
        cols = wn.shape[1]
        in_specs += [_resident((4, d)), _resident((d, cols)), _resident((1, cols))]
        args += [gn, wn, bn]
        out_specs.append(pl.BlockSpec((tm, cols), rows))
        out_shape.append(jax.ShapeDtypeStruct((n, cols), F32))
    outs = pl.pallas_call(
        functools.partial(_post_kernel, with_next=nxt is not None),
        grid=(n // tm,),
        in_specs=in_specs,
        out_specs=out_specs,
        out_shape=out_shape,
        compiler_params=_params(("arbitrary",)),
        name="post_block",
    )(*args)
    return outs if nxt is not None else outs[0]


def _trunk(x, gla_s0, swa_k0, swa_v0, w):
    batch, t_len, d = x.shape
    n = batch * t_len
    tm = min(ROW_TILE, n)
    assert n % tm == 0 and t_len % CHUNK == 0
    xf = x.reshape(n, d)
    og, s_fin = _gla_layer(xf, w["norm_g"][0], w["gla_w_ext"], w["gla_wb"], w["gla_bg"], w["gla_gon"],
                           gla_s0, batch, t_len)
    x1, qkv = _post_block(og, xf, w["norm_g"][0], w["gla_wo"], w["zero_bias"], w["wu"][0], w["wd"][0], tm,
                          nxt=(w["norm_g"][1], w["swa_wi"], w["swa_bi"]))
    if swa_k0 is None and t_len % SWA_LAYER_TILE == 0:
        y = _swa_layer(qkv, x1, w["sinks"], w["norm_g"][1], w["swa_wo"], w["swa_bo"], w["wu"][1], w["wd"][1],
                       t_len, SWA_LAYER_TILE)
    else:
        if swa_k0 is None:
            oa = _swa_attention(qkv, None, None, w["sinks"], batch, t_len, True)
        else:
            kp = swa_k0.reshape(batch * WINDOW, SWA_DKV)
            vp = swa_v0.reshape(batch * WINDOW, SWA_DKV)
            oa = _swa_attention(qkv, kp, vp, w["sinks"], batch, t_len, False)
        y = _post_block(oa, x1, w["norm_g"][1], w["swa_wo"], w["swa_bo"], w["wu"][1], w["wd"][1], tm)
    keep = min(WINDOW, t_len) if swa_k0 is None else swa_k0.shape[1]
    new = min(keep, t_len)
    tail = qkv.reshape(batch, t_len, SWA_COLS)[:, t_len - new:, :]
    k_new = tail[:, :, SWA_DQ:SWA_DQ + SWA_DKV].reshape(batch, new, SWA_KVH, SWA_HD)
    v_new = tail[:, :, SWA_DQ + SWA_DKV:].reshape(batch, new, SWA_KVH, SWA_HD)
    if swa_k0 is None:
        k_cache, v_cache = k_new, v_new
    else:
        k_cache = jnp.concatenate([swa_k0[:, new:], k_new], axis=1)
        v_cache = jnp.concatenate([swa_v0[:, new:], v_new], axis=1)
    return y.reshape(batch, t_len, d), s_fin[None], k_cache[None], v_cache[None]


def kernel(x_prompt, x_sample, state_gla, cache_swa_k, cache_swa_v, gla_w_in, gla_w_gate_a, gla_w_gate_b, gla_b_gate, gla_g_onorm, gla_w_out, swa_w_in, swa_b_in, swa_sinks, swa_w_out, swa_b_out, norm_g, mlp_w_up, mlp_w_down):
    d = D_MODEL
    pad_a = jnp.zeros((d, GLA_GATE_PAD - GLA_GATE_RANK), F32)
    pad_b = jnp.zeros((GLA_GATE_PAD - GLA_GATE_RANK, GLA_DQK), F32)
    w = {
        "norm_g": norm_g,
        "gla_w_ext": jnp.concatenate([gla_w_in[0], gla_w_gate_a[0], pad_a], axis=1).astype(BF16),
        "gla_wb": jnp.concatenate([gla_w_gate_b[0], pad_b], axis=0).astype(BF16),
        "gla_bg": gla_b_gate[0].reshape(1, GLA_DQK),
        "gla_gon": gla_g_onorm[0].reshape(1, GLA_DVT),
        "gla_wo": gla_w_out[0].astype(BF16),
        "zero_bias": jnp.zeros((1, d), F32),
        "swa_wi": swa_w_in[0].astype(BF16),
        "swa_bi": swa_b_in[0].reshape(1, SWA_COLS),
        "sinks": swa_sinks[0],
        "swa_wo": swa_w_out[0].astype(BF16),
        "swa_bo": swa_b_out[0].reshape(1, d),
        "wu": mlp_w_up.astype(BF16),
        "wd": mlp_w_down.astype(BF16),
    }
    y_p, s_p, k_p, v_p = _trunk(x_prompt, None, None, None, w)
    y_s, s_s, k_s, v_s = _trunk(x_sample, state_gla[0], cache_swa_k[0], cache_swa_v[0], w)
    return (y_p, y_s, s_p, s_s, k_p, v_p, k_s, v_s)
```

```python
import functools
import math

import numpy as np
import jax
import jax.numpy as jnp
from jax import lax
from jax.experimental import pallas as pl
from jax.experimental.pallas import tpu as pltpu

F32 = jnp.float32
BF16 = jnp.bfloat16

D_MODEL = 1024
D_FF = 4 * D_MODEL
EPS = 1e-6
CHUNK = 64
LOG2E = math.log2(math.e)
GLA_HEADS = 4
GLA_DK = 128
GLA_DV = 256
GLA_DQK = GLA_HEADS * GLA_DK
GLA_DVT = GLA_HEADS * GLA_DV
GLA_GATE_RANK = 16
GLA_GATE_NORM = 16.0
GLA_GATE_PAD = 128
GLA_PROJ_COLS = 2 * GLA_DQK + 2 * GLA_DVT + GLA_GATE_PAD
GLA_MAX_CHUNK = 256
GLA_PROJ_TILE = 256
GLA_SAFE_DECAY = 60.0
SWA_HD = 64
SWA_QH = 16
SWA_KVH = 4
SWA_GROUP = 4
SWA_DQ = SWA_QH * SWA_HD
SWA_DKV = SWA_KVH * SWA_HD
SWA_COLS = SWA_DQ + 2 * SWA_DKV
SWA_GW = SWA_GROUP * SWA_HD
WINDOW = 128
SWA_NK = 2 * WINDOW
ROW_TILE = 512
SWA_LAYER_TILE = 512
VMEM_LIMIT_BYTES = 60 * 1024 * 1024


def _dot(a, b):
    return jnp.dot(a, b, preferred_element_type=F32)


def _dot_nt(a, b):
    return lax.dot_general(a, b, (((1,), (1,)), ((), ())), preferred_element_type=F32)


def _dot_tn(a, b):
    return lax.dot_general(a, b, (((0,), (0,)), ((), ())), preferred_element_type=F32)


def _rms(x, g):
    return x * lax.rsqrt(jnp.mean(x * x, axis=-1, keepdims=True) + EPS) * g


def _split_bf16(x):
    hi = x.astype(BF16)
    lo = (x - hi.astype(F32)).astype(BF16)
    return hi, lo


def _params(semantics):
    return pltpu.CompilerParams(dimension_semantics=semantics, vmem_limit_bytes=VMEM_LIMIT_BYTES)


def _resident(shape):
    zeros = (0,) * len(shape)
    return pl.BlockSpec(shape, lambda *_: zeros, pipeline_mode=pl.Buffered(1))


def _gla_level_matrices(c):
    levels = []
    s = c
    while s >= 2:
        levels.append(s)
        s //= 2
    t = np.arange(c)[None, :]
    r = np.arange(c)[:, None]
    mats = []
    for s in levels:
        mid = (r // s) * s + s // 2
        mq = (r >= mid) & (t >= mid) & (t <= r)
        mk = (r < mid) & (t > r) & (t <= mid - 1)
        mats.append(mq)
        mats.append(mk)
    return np.stack(mats).astype(np.float32), levels


def _gla_kernel(*refs, c, levels, has_s0, nt):
    if has_s0:
        (x_ref, g_ref, w_ref, wb_ref, bg_ref, gon_ref, tri_ref, lev_ref, s0_ref,
         og_ref, s_ref, a_scr, sprev, proj_a, proj_b) = refs
    else:
        (x_ref, g_ref, w_ref, wb_ref, bg_ref, gon_ref, tri_ref, lev_ref,
         og_ref, s_ref, a_scr, sprev, proj_a, proj_b) = refs
        s0_ref = None
    s = pl.program_id(0)
    first = lax.rem(jnp.maximum(s - 1, 0), nt) == 0

    @pl.when(s == 0)
    def _():
        proj_a[...] = jnp.zeros(proj_a.shape, F32)
        proj_b[...] = jnp.zeros(proj_b.shape, F32)

    scale = GLA_DK ** -0.5
    row = lax.broadcasted_iota(jnp.int32, (c, c), 0)
    col = lax.broadcasted_iota(jnp.int32, (c, c), 1)
    nb = min(c, GLA_DK)

    def projection_pieces(dst):
        h = _rms(x_ref[...], g_ref[0:1, :]).astype(BF16)

        def tile(c0, c1):
            dst[:, c0:c1] = _dot(h, w_ref[:, c0:c1])

        edges = list(range(0, GLA_PROJ_COLS, GLA_PROJ_TILE)) + [GLA_PROJ_COLS]
        return [functools.partial(tile, c0, c1) for c0, c1 in zip(edges[:-1], edges[1:])]

    def step(proj, robust, fillers=()):
        fillers = list(fillers)
        n_slots = 10
        per_slot = -(-len(fillers) // n_slots)

        def fill():
            for _ in range(per_slot):
                if fillers:
                    fillers.pop(0)()

        qk = proj[:, :2 * GLA_DQK]
        v_all = proj[:, 2 * GLA_DQK:2 * GLA_DQK + GLA_DVT].astype(BF16)
        r_all = proj[:, 2 * GLA_DQK + GLA_DVT:2 * GLA_DQK + 2 * GLA_DVT]
        ga = proj[:, 2 * GLA_DQK + 2 * GLA_DVT:].astype(BF16)
        gp = _dot(ga, wb_ref[...]) + bg_ref[...]
        fill()
        lf = (jnp.minimum(gp, 0.0) - jnp.log(1.0 + jnp.exp(-jnp.abs(gp)))) * (1.0 / GLA_GATE_NORM)
        lf_hi, lf_lo = _split_bf16(lf)
        fill()
        tri = tri_ref[...]
        b = _dot(tri, lf_hi) + _dot(tri, lf_lo)
        b_last = b[c - 1:c, :]

        if robust:
            for hd in range(GLA_HEADS):
                cs = slice(hd * GLA_DK, (hd + 1) * GLA_DK)
                ks = slice(GLA_DQK + hd * GLA_DK, GLA_DQK + (hd + 1) * GLA_DK)
                a_scr[hd] = jnp.where(row == col,
                                      _dot_nt((qk[:, cs] * scale).astype(BF16), qk[:, ks].astype(BF16)), 0.0)
            for li, s in enumerate(levels):
                mq = lev_ref[2 * li]
                mk = lev_ref[2 * li + 1]
                dq = _dot(mq, lf_hi) + _dot(mq, lf_lo)
                dk = _dot(mk, lf_hi) + _dot(mk, lf_lo)
                half = s // 2
                sh = s.bit_length() - 1
                same = lax.shift_right_logical(row, sh) == lax.shift_right_logical(col, sh)
                mask = same & ((row & (s - 1)) >= half) & ((col & (s - 1)) < half)
                for hd in range(GLA_HEADS):
                    cs = slice(hd * GLA_DK, (hd + 1) * GLA_DK)
                    ks = slice(GLA_DQK + hd * GLA_DK, GLA_DQK + (hd + 1) * GLA_DK)
                    qq = (qk[:, cs] * scale * jnp.exp(dq[:, cs])).astype(BF16)
                    kk = (qk[:, ks] * jnp.exp(dk[:, cs])).astype(BF16)
                    a_scr[hd] = a_scr[hd] + jnp.where(mask, _dot_nt(qq, kk), 0.0)

        heads = range(GLA_HEADS)
        cs = [slice(hd * GLA_DK, (hd + 1) * GLA_DK) for hd in heads]
        ks = [slice(GLA_DQK + hd * GLA_DK, GLA_DQK + (hd + 1) * GLA_DK) for hd in heads]
        vs = [slice(hd * GLA_DV, (hd + 1) * GLA_DV) for hd in heads]
        qe = [(qk[:, cs[hd]] * scale * jnp.exp(b[:, cs[hd]])).astype(BF16) for hd in heads]
        fill()
        if robust:
            s_old = [sprev[hd] for hd in heads]
            a = [a_scr[hd].astype(BF16) for hd in heads]
        else:
            init = [s0_ref[0, hd] if has_s0 else jnp.zeros((GLA_DK, GLA_DV), F32) for hd in heads]
            s_old = [jnp.where(first, init[hd], s_ref[0, hd]) for hd in heads]
            for hd in heads:
                sprev[hd] = s_old[hd]
            ke = [(qk[:, ks[hd]] * jnp.exp(-b[:, cs[hd]])).astype(BF16) for hd in heads]
            fill()
            a = [jnp.where(row >= col, _dot_nt(qe[hd], ke[hd]), 0.0).astype(BF16) for hd in heads]
        fill()
        o = [_dot(qe[hd], s_old[hd].astype(BF16)) + _dot(a[hd], v_all[:, vs[hd]]) for hd in heads]
        fill()
        if not robust:
            kd = [(qk[:, ks[hd]] * jnp.exp(b_last[:, cs[hd]] - b[:, cs[hd]])).astype(BF16) for hd in heads]
            for hd in heads:
                dec = jnp.exp(jnp.transpose(b[c - nb:c, cs[hd]])[:, nb - 1:nb])
                s_ref[0, hd] = s_old[hd] * dec + _dot_tn(kd[hd], v_all[:, vs[hd]])
        for hd in heads:
            fill()
            on = o[hd] * lax.rsqrt(jnp.mean(o[hd] * o[hd], axis=-1, keepdims=True) + EPS) * gon_ref[:, vs[hd]]
            hr = 0.5 * r_all[:, vs[hd]]
            og_ref[:, vs[hd]] = (on * (hr + hr * jnp.tanh(hr))).astype(BF16)
        while fillers:
            fillers.pop(0)()
        return jnp.min(b)

    def body(src, dst):
        b_min = step(src, False, projection_pieces(dst))

        @pl.when(b_min < -GLA_SAFE_DECAY)
        def _():
            step(src, True)

    @pl.when(lax.rem(s, 2) == 0)
    def _():
        body(proj_b, proj_a)

    @pl.when(lax.rem(s, 2) == 1)
    def _():
        body(proj_a, proj_b)


def _gla_layer(x, g4, w_ext, wb, bg, gon, s0, batch, t_len):
    d = x.shape[1]
    c = min(t_len, GLA_MAX_CHUNK)
    nt = t_len // c
    n_chunks = batch * nt
    lev_np, levels = _gla_level_matrices(c)
    lev = jnp.asarray(lev_np, BF16)
    tri = jnp.asarray(np.tril(np.ones((c, c), np.float32)), BF16)
    has_s0 = s0 is not None
    p_idx = lambda s: jnp.minimum(s, n_chunks - 1)
    t_idx = lambda s: jnp.maximum(s - 1, 0)
    in_specs = [
        pl.BlockSpec((c, d), lambda s: (p_idx(s), 0)),
        _resident((4, d)),
        _resident((d, GLA_PROJ_COLS)),
        _resident((GLA_GATE_PAD, GLA_DQK)),
        _resident((1, GLA_DQK)),
        _resident((1, GLA_DVT)),
        _resident((c, c)),
        _resident((2 * len(levels), c, c)),
    ]
    args = [x, g4, w_ext, wb, bg, gon, tri, lev]
    state_spec = pl.BlockSpec((1, GLA_HEADS, GLA_DK, GLA_DV), lambda s: (t_idx(s) // nt, 0, 0, 0))
    if has_s0:
        in_specs.append(state_spec)
        args.append(s0)
    og, s_fin = pl.pallas_call(
        functools.partial(_gla_kernel, c=c, levels=tuple(levels), has_s0=has_s0, nt=nt),
        grid=(n_chunks + 1,),
        in_specs=in_specs,
        out_specs=[
            pl.BlockSpec((c, GLA_DVT), lambda s: (t_idx(s), 0)),
            state_spec,
        ],
        out_shape=[
            jax.ShapeDtypeStruct((batch * t_len, GLA_DVT), BF16),
            jax.ShapeDtypeStruct((batch, GLA_HEADS, GLA_DK, GLA_DV), F32),
        ],
        scratch_shapes=[
            pltpu.VMEM((GLA_HEADS, c, c), F32),
            pltpu.VMEM((GLA_HEADS, GLA_DK, GLA_DV), F32),
            pltpu.VMEM((c, GLA_PROJ_COLS), F32),
            pltpu.VMEM((c, GLA_PROJ_COLS), F32),
        ],
        compiler_params=_params(("arbitrary",)),
        name="gla_layer",
    )(*args)
    return og, s_fin


def _attend(qkv_ref, r0, tq, k_prev, v_prev, prev_valid, sink_ref, kblk, vblk, bias_scr, o_ref):
    for piece in _attend_pieces(qkv_ref, r0, tq, k_prev, v_prev, prev_valid, sink_ref, kblk, vblk, bias_scr, o_ref):
        piece()


def _attend_pieces(qkv_ref, r0, tq, k_prev, v_prev, prev_valid, sink_ref, kblk, vblk, bias_scr, o_ref):
    def prologue():
        pad = SWA_NK - WINDOW - tq
        k_parts = [k_prev(), qkv_ref[r0:r0 + tq, SWA_DQ:SWA_DQ + SWA_DKV]]
        v_parts = [v_prev(), qkv_ref[r0:r0 + tq, SWA_DQ + SWA_DKV:]]
        if pad:
            k_parts.append(jnp.zeros((pad, SWA_DKV), F32))
            v_parts.append(jnp.zeros((pad, SWA_DKV), F32))
        kt = jnp.concatenate(k_parts, axis=0).T.astype(BF16)
        vc = jnp.concatenate(v_parts, axis=0).astype(BF16)
        for kh in range(SWA_KVH):
            for g in range(SWA_GROUP):
                kblk[kh, g * SWA_HD:(g + 1) * SWA_HD, g * SWA_NK:(g + 1) * SWA_NK] = kt[kh * SWA_HD:(kh + 1) * SWA_HD, :]
                vblk[kh, g * SWA_NK:(g + 1) * SWA_NK, g * SWA_HD:(g + 1) * SWA_HD] = vc[:, kh * SWA_HD:(kh + 1) * SWA_HD]
        qrow = lax.broadcasted_iota(jnp.int32, (tq, SWA_NK), 0)
        kcol = lax.broadcasted_iota(jnp.int32, (tq, SWA_NK), 1)
        lo = lax.shift_right_logical(qrow, 6) * CHUNK
        mask = (kcol >= lo) & (kcol < lo + WINDOW + CHUNK) & (kcol < WINDOW + tq)
        if prev_valid is not None:
            mask = mask & jnp.logical_or(kcol >= WINDOW, prev_valid)
        bias_scr[...] = jnp.where(mask, 0.0, -jnp.inf)

    def head(kh):
        lane = lax.broadcasted_iota(jnp.int32, (tq, SWA_GW), 1)
        q4 = (qkv_ref[r0:r0 + tq, kh * SWA_GW:(kh + 1) * SWA_GW] * (SWA_HD ** -0.5 * LOG2E)).astype(BF16)
        acc = None
        linv = []
        for g in range(SWA_GROUP):
            sg = _dot(q4, kblk[kh, :, g * SWA_NK:(g + 1) * SWA_NK]) + bias_scr[...]
            sink = sink_ref[kh * SWA_GROUP + g] * LOG2E
            m = jnp.maximum(jnp.max(sg, axis=-1, keepdims=True), sink)
            p = jnp.exp2(sg - m)
            linv.append(1.0 / (jnp.sum(p, axis=-1, keepdims=True) + jnp.exp2(sink - m)))
            part = _dot(p.astype(BF16), vblk[kh, g * SWA_NK:(g + 1) * SWA_NK, :])
            acc = part if acc is None else acc + part
        norm = jnp.where(lane < SWA_HD, linv[0],
                         jnp.where(lane < 2 * SWA_HD, linv[1],
                                   jnp.where(lane < 3 * SWA_HD, linv[2], linv[3])))
        o_ref[r0:r0 + tq, kh * SWA_GW:(kh + 1) * SWA_GW] = (acc * norm).astype(BF16)

    return [prologue] + [functools.partial(head, kh) for kh in range(SWA_KVH)]


def _swa_kernel(sink_ref, qkv_ref, kprev_ref, vprev_ref, o_ref, kblk, vblk, bias_scr, *, tq, prev_always_valid):
    t = pl.program_id(1)

    @pl.when(jnp.logical_and(pl.program_id(0) == 0, t == 0))
    def _():
        kblk[...] = jnp.zeros(kblk.shape, BF16)
        vblk[...] = jnp.zeros(vblk.shape, BF16)

    _attend(qkv_ref, 0, tq, lambda: kprev_ref[...], lambda: vprev_ref[...],
            None if prev_always_valid else t > 0, sink_ref, kblk, vblk, bias_scr, o_ref)


def _swa_attention(qkv, k_prev, v_prev, sinks, batch, t_len, prev_from_qkv):
    tq = min(t_len, WINDOW)
    nt = t_len // tq
    if prev_from_qkv:
        assert tq == WINDOW
        kspec = pl.BlockSpec((WINDOW, SWA_DKV),
                             lambda b, t: (jnp.maximum(b * nt + t - 1, 0), SWA_DQ // SWA_DKV))
        vspec = pl.BlockSpec((WINDOW, SWA_DKV),
                             lambda b, t: (jnp.maximum(b * nt + t - 1, 0), SWA_DQ // SWA_DKV + 1))
        k_prev = v_prev = qkv
    else:
        kspec = pl.BlockSpec((WINDOW, SWA_DKV), lambda b, t: (b, 0))
        vspec = pl.BlockSpec((WINDOW, SWA_DKV), lambda b, t: (b, 0))
    return pl.pallas_call(
        functools.partial(_swa_kernel, tq=tq, prev_always_valid=not prev_from_qkv),
        grid=(batch, nt),
        in_specs=[
            pl.BlockSpec(memory_space=pltpu.SMEM),
            pl.BlockSpec((tq, SWA_COLS), lambda b, t: (b * nt + t, 0)),
            kspec,
            vspec,
        ],
        out_specs=pl.BlockSpec((tq, SWA_DQ), lambda b, t: (b * nt + t, 0)),
        out_shape=jax.ShapeDtypeStruct((batch * t_len, SWA_DQ), BF16),
        scratch_shapes=[
            pltpu.VMEM((SWA_KVH, SWA_GW, SWA_GROUP * SWA_NK), BF16),
            pltpu.VMEM((SWA_KVH, SWA_GROUP * SWA_NK, SWA_GW), BF16),
            pltpu.VMEM((tq, SWA_NK), F32),
        ],
        compiler_params=_params(("arbitrary", "arbitrary")),
        name="swa_attention",
    )(sinks, qkv, k_prev, v_prev)


FF_CHUNK = 1024
POST_ROW_GROUPS = 2
LAYER_FF_CHUNK = 512


def _post_rows(o, x, g_ref, wo_ref, bo_ref, wu_ref, wd_ref):
    m = o.shape[0]
    ng = POST_ROW_GROUPS if m % (POST_ROW_GROUPS * 128) == 0 else 1
    rows = [slice(i * (m // ng), (i + 1) * (m // ng)) for i in range(ng)]
    a = [_dot(o[r], wo_ref[...]) + bo_ref[...] for r in rows]
    x1 = [x[r] + _rms(a[i], g_ref[1:2, :]) for i, r in enumerate(rows)]
    h = [_rms(x1[i], g_ref[2:3, :]).astype(BF16) for i in range(ng)]
    acc = [None] * ng
    for j in range(D_FF // FF_CHUNK):
        for i in range(ng):
            u = _dot(h[i], wu_ref[:, j * FF_CHUNK:(j + 1) * FF_CHUNK])
            u = jnp.maximum(u, 0.0)
            u = (u * u).astype(BF16)
            part = _dot(u, wd_ref[j * FF_CHUNK:(j + 1) * FF_CHUNK, :])
            acc[i] = part if acc[i] is None else acc[i] + part
    y = [x1[i] + _rms(acc[i], g_ref[3:4, :]) for i in range(ng)]
    return y[0] if ng == 1 else jnp.concatenate(y, axis=0)


def _post_kernel(*refs, with_next):
    if with_next:
        o_ref, x_ref, g_ref, wo_ref, bo_ref, wu_ref, wd_ref, gn_ref, wn_ref, bn_ref, y_ref, p_ref = refs
    else:
        o_ref, x_ref, g_ref, wo_ref, bo_ref, wu_ref, wd_ref, y_ref = refs
    y = _post_rows(o_ref[...], x_ref[...], g_ref, wo_ref, bo_ref, wu_ref, wd_ref)
    y_ref[...] = y
    if with_next:
        hn = _rms(y, gn_ref[0:1, :]).astype(BF16)
        p_ref[...] = _dot(hn, wn_ref[...]) + bn_ref[...]


def _swa_layer_kernel(sink_ref, qkv_ref, kprev_ref, vprev_ref, x_ref, g_ref, wo_ref, bo_ref, wu_ref, wd_ref,
                      y_ref, kblk, vblk, bias_scr, o_scr, a_scr, *, n_tiles, tile, blocks_per_seq):
    s = pl.program_id(0)

    @pl.when(s == 0)
    def _():
        kblk[...] = jnp.zeros(kblk.shape, BF16)
        vblk[...] = jnp.zeros(vblk.shape, BF16)
        o_scr[...] = jnp.zeros(o_scr.shape, BF16)

    a_tile = jnp.minimum(s, n_tiles - 1)
    n_sub = tile // WINDOW

    pieces = []
    for j in range(n_sub):
        r0 = j * WINDOW
        if j == 0:
            k_prev, v_prev = (lambda: kprev_ref[...]), (lambda: vprev_ref[...])
            prev_valid = lax.rem(a_tile * n_sub, blocks_per_seq) != 0
        else:
            k_prev = functools.partial(lambda r: qkv_ref[r - WINDOW:r, SWA_DQ:SWA_DQ + SWA_DKV], r0)
            v_prev = functools.partial(lambda r: qkv_ref[r - WINDOW:r, SWA_DQ + SWA_DKV:], r0)
            prev_valid = None
        pieces += _attend_pieces(qkv_ref, r0, WINDOW, k_prev, v_prev, prev_valid,
                                 sink_ref, kblk, vblk, bias_scr, a_scr)

    n_ff = D_FF // LAYER_FF_CHUNK
    slots = n_ff
    per_slot = -(-len(pieces) // slots)

    def run_pieces():
        for _ in range(per_slot):
            if pieces:
                pieces.pop(0)()

    a = _dot(o_scr[...], wo_ref[...]) + bo_ref[...]
    x1 = x_ref[...] + _rms(a, g_ref[1:2, :])
    h = _rms(x1, g_ref[2:3, :]).astype(BF16)
    run_pieces()
    acc = None
    for j in range(n_ff):
        u = _dot(h, wu_ref[:, j * LAYER_FF_CHUNK:(j + 1) * LAYER_FF_CHUNK])
        u = jnp.maximum(u, 0.0)
        u = (u * u).astype(BF16)
        part = _dot(u, wd_ref[j * LAYER_FF_CHUNK:(j + 1) * LAYER_FF_CHUNK, :])
        acc = part if acc is None else acc + part
        if j + 1 < n_ff:
            run_pieces()
    assert not pieces
    y_ref[...] = x1 + _rms(acc, g_ref[3:4, :])
    o_scr[...] = a_scr[...]


def _swa_layer(qkv, x, sinks, g4, wo, bo, wu, wd, t_len, tile):
    n, d = x.shape
    assert t_len % tile == 0 and tile % WINDOW == 0
    n_tiles = n // tile
    per = tile // WINDOW
    a_idx = lambda s: jnp.minimum(s, n_tiles - 1)
    p_idx = lambda s: jnp.maximum(s - 1, 0)
    return pl.pallas_call(
        functools.partial(_swa_layer_kernel, n_tiles=n_tiles, tile=tile, blocks_per_seq=t_len // WINDOW),
        grid=(n_tiles + 1,),
        in_specs=[
            pl.BlockSpec(memory_space=pltpu.SMEM),
            pl.BlockSpec((tile, SWA_COLS), lambda s: (a_idx(s), 0)),
            pl.BlockSpec((WINDOW, SWA_DKV), lambda s: (jnp.maximum(a_idx(s) * per - 1, 0), SWA_DQ // SWA_DKV)),
            pl.BlockSpec((WINDOW, SWA_DKV), lambda s: (jnp.maximum(a_idx(s) * per - 1, 0), SWA_DQ // SWA_DKV + 1)),
            pl.BlockSpec((tile, d), lambda s: (p_idx(s), 0)),
            _resident((4, d)),
            _resident((d, d)),
            _resident((1, d)),
            _resident((d, D_FF)),
            _resident((D_FF, d)),
        ],
        out_specs=pl.BlockSpec((tile, d), lambda s: (p_idx(s), 0)),
        out_shape=jax.ShapeDtypeStruct((n, d), F32),
        scratch_shapes=[
            pltpu.VMEM((SWA_KVH, SWA_GW, SWA_GROUP * SWA_NK), BF16),
            pltpu.VMEM((SWA_KVH, SWA_GROUP * SWA_NK, SWA_GW), BF16),
            pltpu.VMEM((WINDOW, SWA_NK), F32),
            pltpu.VMEM((tile, SWA_DQ), BF16),
            pltpu.VMEM((tile, SWA_DQ), BF16),
        ],
        compiler_params=_params(("arbitrary",)),
        name="swa_layer",
    )(sinks, qkv, qkv, qkv, x, g4, wo, bo, wu, wd)


def _post_block(o, x, g4, wo, bo, wu, wd, tm, nxt=None):
    n, d = x.shape
    rows = lambda i: (i, 0)
    in_specs = [
        pl.BlockSpec((tm, d), rows),
        pl.BlockSpec((tm, d), rows),
        _resident((4, d)),
        _resident((d, d)),
        _resident((1, d)),
        _resident((d, D_FF)),
        _resident((D_FF, d)),
    ]
    args = [o, x, g4, wo, bo, wu, wd]
    out_specs = [pl.BlockSpec((tm, d), rows)]
    out_shape = [jax.ShapeDtypeStruct((n, d), F32)]
    if nxt is not None:
        gn, wn, bn = nxt
        cols = wn.shape[1]
        in_specs += [_resident((4, d)), _resident((d, cols)), _resident((1, cols))]
        args += [gn, wn, bn]
        out_specs.append(pl.BlockSpec((tm, cols), rows))
        out_shape.append(jax.ShapeDtypeStruct((n, cols), F32))
    outs = pl.pallas_call(
        functools.partial(_post_kernel, with_next=nxt is not None),
        grid=(n // tm,),
        in_specs=in_specs,
        out_specs=out_specs,
        out_shape=out_shape,
        compiler_params=_params(("arbitrary",)),
        name="post_block",
    )(*args)
    return outs if nxt is not None else outs[0]


def _trunk(x, gla_s0, swa_k0, swa_v0, w):
    batch, t_len, d = x.shape
    n = batch * t_len
    tm = min(ROW_TILE, n)
    assert n % tm == 0 and t_len % CHUNK == 0
    xf = x.reshape(n, d)
    og, s_fin = _gla_layer(xf, w["norm_g"][0], w["gla_w_ext"], w["gla_wb"], w["gla_bg"], w["gla_gon"],
                           gla_s0, batch, t_len)
    x1, qkv = _post_block(og, xf, w["norm_g"][0], w["gla_wo"], w["zero_bias"], w["wu"][0], w["wd"][0], tm,
                          nxt=(w["norm_g"][1], w["swa_wi"], w["swa_bi"]))
    if swa_k0 is None and t_len % SWA_LAYER_TILE == 0:
        y = _swa_layer(qkv, x1, w["sinks"], w["norm_g"][1], w["swa_wo"], w["swa_bo"], w["wu"][1], w["wd"][1],
                       t_len, SWA_LAYER_TILE)
    else:
        if swa_k0 is None:
            oa = _swa_attention(qkv, None, None, w["sinks"], batch, t_len, True)
        else:
            kp = swa_k0.reshape(batch * WINDOW, SWA_DKV)
            vp = swa_v0.reshape(batch * WINDOW, SWA_DKV)
            oa = _swa_attention(qkv, kp, vp, w["sinks"], batch, t_len, False)
        y = _post_block(oa, x1, w["norm_g"][1], w["swa_wo"], w["swa_bo"], w["wu"][1], w["wd"][1], tm)
    keep = min(WINDOW, t_len) if swa_k0 is None else swa_k0.shape[1]
    new = min(keep, t_len)
    tail = qkv.reshape(batch, t_len, SWA_COLS)[:, t_len - new:, :]
    k_new = tail[:, :, SWA_DQ:SWA_DQ + SWA_DKV].reshape(batch, new, SWA_KVH, SWA_HD)
    v_new = tail[:, :, SWA_DQ + SWA_DKV:].reshape(batch, new, SWA_KVH, SWA_HD)
    if swa_k0 is None:
        k_cache, v_cache = k_new, v_new
    else:
        k_cache = jnp.concatenate([swa_k0[:, new:], k_new], axis=1)
        v_cache = jnp.concatenate([swa_v0[:, new:], v_new], axis=1)
    return y.reshape(batch, t_len, d), s_fin[None], k_cache[None], v_cache[None]


def kernel(x_prompt, x_sample, state_gla, cache_swa_k, cache_swa_v, gla_w_in, gla_w_gate_a, gla_w_gate_b, gla_b_gate, gla_g_onorm, gla_w_out, swa_w_in, swa_b_in, swa_sinks, swa_w_out, swa_b_out, norm_g, mlp_w_up, mlp_w_down):
    d = D_MODEL
    pad_a = jnp.zeros((d, GLA_GATE_PAD - GLA_GATE_RANK), F32)
    pad_b = jnp.zeros((GLA_GATE_PAD - GLA_GATE_RANK, GLA_DQK), F32)
    w = {
        "norm_g": norm_g,
        "gla_w_ext": jnp.concatenate([gla_w_in[0], gla_w_gate_a[0], pad_a], axis=1).astype(BF16),
        "gla_wb": jnp.concatenate([gla_w_gate_b[0], pad_b], axis=0).astype(BF16),
        "gla_bg": gla_b_gate[0].reshape(1, GLA_DQK),
        "gla_gon": gla_g_onorm[0].reshape(1, GLA_DVT),
        "gla_wo": gla_w_out[0].astype(BF16),
        "zero_bias": jnp.zeros((1, d), F32),
        "swa_wi": swa_w_in[0].astype(BF16),
        "swa_bi": swa_b_in[0].reshape(1, SWA_COLS),
        "sinks": swa_sinks[0],
        "swa_wo": swa_w_out[0].astype(BF16),
        "swa_bo": swa_b_out[0].reshape(1, d),
        "wu": mlp_w_up.astype(BF16),
        "wd": mlp_w_down.astype(BF16),
    }
    y_p, s_p, k_p, v_p = _trunk(x_prompt, None, None, None, w)
    y_s, s_s, k_s, v_s = _trunk(x_sample, state_gla[0], cache_swa_k[0], cache_swa_v[0], w)
    return (y_p, y_s, s_p, s_s, k_p, v_p, k_s, v_s)
```

```python
import functools
import math

import numpy as np
import jax
import jax.numpy as jnp
from jax import lax
from jax.experimental import pallas as pl
from jax.experimental.pallas import tpu as pltpu

F32 = jnp.float32
BF16 = jnp.bfloat16

D_MODEL = 1024
D_FF = 4 * D_MODEL
EPS = 1e-6
CHUNK = 64
LOG2E = math.log2(math.e)
GLA_HEADS = 4
GLA_DK = 128
GLA_DV = 256
GLA_DQK = GLA_HEADS * GLA_DK
GLA_DVT = GLA_HEADS * GLA_DV
GLA_GATE_RANK = 16
GLA_GATE_NORM = 16.0
GLA_GATE_PAD = 128
GLA_PROJ_COLS = 2 * GLA_DQK + 2 * GLA_DVT + GLA_GATE_PAD
GLA_MAX_CHUNK = 256
GLA_PROJ_TILE = 256
GLA_SAFE_DECAY = 60.0
SWA_HD = 64
SWA_QH = 16
SWA_KVH = 4
SWA_GROUP = 4
SWA_DQ = SWA_QH * SWA_HD
SWA_DKV = SWA_KVH * SWA_HD
SWA_COLS = SWA_DQ + 2 * SWA_DKV
SWA_GW = SWA_GROUP * SWA_HD
WINDOW = 128
SWA_NK = 2 * WINDOW
ROW_TILE = 512
SWA_LAYER_TILE = 512
VMEM_LIMIT_BYTES = 60 * 1024 * 1024


def _dot(a, b):
    return jnp.dot(a, b, preferred_element_type=F32)


def _dot_nt(a, b):
    return lax.dot_general(a, b, (((1,), (1,)), ((), ())), preferred_element_type=F32)


def _dot_tn(a, b):
    return lax.dot_general(a, b, (((0,), (0,)), ((), ())), preferred_element_type=F32)


def _rms(x, g):
    return x * lax.rsqrt(jnp.mean(x * x, axis=-1, keepdims=True) + EPS) * g


def _split_bf16(x):
    hi = x.astype(BF16)
    lo = (x - hi.astype(F32)).astype(BF16)
    return hi, lo


def _params(semantics):
    return pltpu.CompilerParams(dimension_semantics=semantics, vmem_limit_bytes=VMEM_LIMIT_BYTES)


def _resident(shape):
    zeros = (0,) * len(shape)
    return pl.BlockSpec(shape, lambda *_: zeros, pipeline_mode=pl.Buffered(1))


def _gla_level_matrices(c):
    levels = []
    s = c
    while s >= 2:
        levels.append(s)
        s //= 2
    t = np.arange(c)[None, :]
    r = np.arange(c)[:, None]
    mats = []
    for s in levels:
        mid = (r // s) * s + s // 2
        mq = (r >= mid) & (t >= mid) & (t <= r)
        mk = (r < mid) & (t > r) & (t <= mid - 1)
        mats.append(mq)
        mats.append(mk)
    return np.stack(mats).astype(np.float32), levels


def _gla_kernel(*refs, c, levels, has_s0, nt):
    if has_s0:
        (x_ref, g_ref, w_ref, wb_ref, bg_ref, gon_ref, tri_ref, lev_ref, s0_ref,
         og_ref, s_ref, a_scr, sprev, proj_a, proj_b) = refs
    else:
        (x_ref, g_ref, w_ref, wb_ref, bg_ref, gon_ref, tri_ref, lev_ref,
         og_ref, s_ref, a_scr, sprev, proj_a, proj_b) = refs
        s0_ref = None
    s = pl.program_id(0)
    first = lax.rem(jnp.maximum(s - 1, 0), nt) == 0

    @pl.when(s == 0)
    def _():
        proj_a[...] = jnp.zeros(proj_a.shape, F32)
        proj_b[...] = jnp.zeros(proj_b.shape, F32)

    scale = GLA_DK ** -0.5
    row = lax.broadcasted_iota(jnp.int32, (c, c), 0)
    col = lax.broadcasted_iota(jnp.int32, (c, c), 1)
    nb = min(c, GLA_DK)

    def projection_pieces(dst):
        h = _rms(x_ref[...], g_ref[0:1, :]).astype(BF16)

        def tile(c0, c1):
            dst[:, c0:c1] = _dot(h, w_ref[:, c0:c1])

        edges = list(range(0, GLA_PROJ_COLS, GLA_PROJ_TILE)) + [GLA_PROJ_COLS]
        return [functools.partial(tile, c0, c1) for c0, c1 in zip(edges[:-1], edges[1:])]

    def step(proj, robust, fillers=()):
        fillers = list(fillers)
        n_slots = 13
        per_slot = -(-len(fillers) // n_slots)

        def fill():
            for _ in range(per_slot):
                if fillers:
                    fillers.pop(0)()

        qk = proj[:, :2 * GLA_DQK]
        v_all = proj[:, 2 * GLA_DQK:2 * GLA_DQK + GLA_DVT].astype(BF16)
        r_all = proj[:, 2 * GLA_DQK + GLA_DVT:2 * GLA_DQK + 2 * GLA_DVT]
        ga = proj[:, 2 * GLA_DQK + 2 * GLA_DVT:].astype(BF16)
        gp = _dot(ga, wb_ref[...]) + bg_ref[...]
        fill()
        lf = (jnp.minimum(gp, 0.0) - jnp.log(1.0 + jnp.exp(-jnp.abs(gp)))) * (1.0 / GLA_GATE_NORM)
        lf_hi, lf_lo = _split_bf16(lf)
        fill()
        tri = tri_ref[...]
        b = _dot(tri, lf_hi) + _dot(tri, lf_lo)
        b_last = b[c - 1:c, :]
        fill()

        if robust:
            for hd in range(GLA_HEADS):
                cs = slice(hd * GLA_DK, (hd + 1) * GLA_DK)
                ks = slice(GLA_DQK + hd * GLA_DK, GLA_DQK + (hd + 1) * GLA_DK)
                a_scr[hd] = jnp.where(row == col,
                                      _dot_nt((qk[:, cs] * scale).astype(BF16), qk[:, ks].astype(BF16)), 0.0)
            for li, s in enumerate(levels):
                mq = lev_ref[2 * li]
                mk = lev_ref[2 * li + 1]
                dq = _dot(mq, lf_hi) + _dot(mq, lf_lo)
                dk = _dot(mk, lf_hi) + _dot(mk, lf_lo)
                half = s // 2
                sh = s.bit_length() - 1
                same = lax.shift_right_logical(row, sh) == lax.shift_right_logical(col, sh)
                mask = same & ((row & (s - 1)) >= half) & ((col & (s - 1)) < half)
                for hd in range(GLA_HEADS):
                    cs = slice(hd * GLA_DK, (hd + 1) * GLA_DK)
                    ks = slice(GLA_DQK + hd * GLA_DK, GLA_DQK + (hd + 1) * GLA_DK)
                    qq = (qk[:, cs] * scale * jnp.exp(dq[:, cs])).astype(BF16)
                    kk = (qk[:, ks] * jnp.exp(dk[:, cs])).astype(BF16)
                    a_scr[hd] = a_scr[hd] + jnp.where(mask, _dot_nt(qq, kk), 0.0)

        heads = range(GLA_HEADS)
        cs = [slice(hd * GLA_DK, (hd + 1) * GLA_DK) for hd in heads]
        ks = [slice(GLA_DQK + hd * GLA_DK, GLA_DQK + (hd + 1) * GLA_DK) for hd in heads]
        vs = [slice(hd * GLA_DV, (hd + 1) * GLA_DV) for hd in heads]
        qe = [(qk[:, cs[hd]] * scale * jnp.exp(b[:, cs[hd]])).astype(BF16) for hd in heads]
        fill()
        if robust:
            s_old = [sprev[hd] for hd in heads]
            a = [a_scr[hd].astype(BF16) for hd in heads]
        else:
            init = [s0_ref[0, hd] if has_s0 else jnp.zeros((GLA_DK, GLA_DV), F32) for hd in heads]
            s_old = [jnp.where(first, init[hd], s_ref[0, hd]) for hd in heads]
            for hd in heads:
                sprev[hd] = s_old[hd]
            ke = [(qk[:, ks[hd]] * jnp.exp(-b[:, cs[hd]])).astype(BF16) for hd in heads]
            fill()
            a = [jnp.where(row >= col, _dot_nt(qe[hd], ke[hd]), 0.0).astype(BF16) for hd in heads]
        fill()
        o = [_dot(qe[hd], s_old[hd].astype(BF16)) + _dot(a[hd], v_all[:, vs[hd]]) for hd in heads]
        fill()
        if not robust:
            kd = [(qk[:, ks[hd]] * jnp.exp(b_last[:, cs[hd]] - b[:, cs[hd]])).astype(BF16) for hd in heads]
            fill()
            fill()
            for hd in heads:
                dec = jnp.exp(jnp.transpose(b[c - nb:c, cs[hd]])[:, nb - 1:nb])
                s_ref[0, hd] = s_old[hd] * dec + _dot_tn(kd[hd], v_all[:, vs[hd]])
        for hd in heads:
            fill()
            on = o[hd] * lax.rsqrt(jnp.mean(o[hd] * o[hd], axis=-1, keepdims=True) + EPS) * gon_ref[:, vs[hd]]
            hr = 0.5 * r_all[:, vs[hd]]
            og_ref[:, vs[hd]] = (on * (hr + hr * jnp.tanh(hr))).astype(BF16)
        while fillers:
            fillers.pop(0)()
        return jnp.min(b)

    def body(src, dst):
        b_min = step(src, False, projection_pieces(dst))

        @pl.when(b_min < -GLA_SAFE_DECAY)
        def _():
            step(src, True)

    @pl.when(lax.rem(s, 2) == 0)
    def _():
        body(proj_b, proj_a)

    @pl.when(lax.rem(s, 2) == 1)
    def _():
        body(proj_a, proj_b)


def _gla_layer(x, g4, w_ext, wb, bg, gon, s0, batch, t_len):
    d = x.shape[1]
    c = min(t_len, GLA_MAX_CHUNK)
    nt = t_len // c
    n_chunks = batch * nt
    lev_np, levels = _gla_level_matrices(c)
    lev = jnp.asarray(lev_np, BF16)
    tri = jnp.asarray(np.tril(np.ones((c, c), np.float32)), BF16)
    has_s0 = s0 is not None
    p_idx = lambda s: jnp.minimum(s, n_chunks - 1)
    t_idx = lambda s: jnp.maximum(s - 1, 0)
    in_specs = [
        pl.BlockSpec((c, d), lambda s: (p_idx(s), 0)),
        _resident((4, d)),
        _resident((d, GLA_PROJ_COLS)),
        _resident((GLA_GATE_PAD, GLA_DQK)),
        _resident((1, GLA_DQK)),
        _resident((1, GLA_DVT)),
        _resident((c, c)),
        _resident((2 * len(levels), c, c)),
    ]
    args = [x, g4, w_ext, wb, bg, gon, tri, lev]
    state_spec = pl.BlockSpec((1, GLA_HEADS, GLA_DK, GLA_DV), lambda s: (t_idx(s) // nt, 0, 0, 0))
    if has_s0:
        in_specs.append(state_spec)
        args.append(s0)
    og, s_fin = pl.pallas_call(
        functools.partial(_gla_kernel, c=c, levels=tuple(levels), has_s0=has_s0, nt=nt),
        grid=(n_chunks + 1,),
        in_specs=in_specs,
        out_specs=[
            pl.BlockSpec((c, GLA_DVT), lambda s: (t_idx(s), 0)),
            state_spec,
        ],
        out_shape=[
            jax.ShapeDtypeStruct((batch * t_len, GLA_DVT), BF16),
            jax.ShapeDtypeStruct((batch, GLA_HEADS, GLA_DK, GLA_DV), F32),
        ],
        scratch_shapes=[
            pltpu.VMEM((GLA_HEADS, c, c), F32),
            pltpu.VMEM((GLA_HEADS, GLA_DK, GLA_DV), F32),
            pltpu.VMEM((c, GLA_PROJ_COLS), F32),
            pltpu.VMEM((c, GLA_PROJ_COLS), F32),
        ],
        compiler_params=_params(("arbitrary",)),
        name="gla_layer",
    )(*args)
    return og, s_fin


def _attend(qkv_ref, r0, tq, k_prev, v_prev, prev_valid, sink_ref, kblk, vblk, bias_scr, o_ref):
    for piece in _attend_pieces(qkv_ref, r0, tq, k_prev, v_prev, prev_valid, sink_ref, kblk, vblk, bias_scr, o_ref):
        piece()


def _attend_pieces(qkv_ref, r0, tq, k_prev, v_prev, prev_valid, sink_ref, kblk, vblk, bias_scr, o_ref):
    def prologue():
        pad = SWA_NK - WINDOW - tq
        k_parts = [k_prev(), qkv_ref[r0:r0 + tq, SWA_DQ:SWA_DQ + SWA_DKV]]
        v_parts = [v_prev(), qkv_ref[r0:r0 + tq, SWA_DQ + SWA_DKV:]]
        if pad:
            k_parts.append(jnp.zeros((pad, SWA_DKV), F32))
            v_parts.append(jnp.zeros((pad, SWA_DKV), F32))
        kt = jnp.concatenate(k_parts, axis=0).T.astype(BF16)
        vc = jnp.concatenate(v_parts, axis=0).astype(BF16)
        for kh in range(SWA_KVH):
            for g in range(SWA_GROUP):
                kblk[kh, g * SWA_HD:(g + 1) * SWA_HD, g * SWA_NK:(g + 1) * SWA_NK] = kt[kh * SWA_HD:(kh + 1) * SWA_HD, :]
                vblk[kh, g * SWA_NK:(g + 1) * SWA_NK, g * SWA_HD:(g + 1) * SWA_HD] = vc[:, kh * SWA_HD:(kh + 1) * SWA_HD]
        qrow = lax.broadcasted_iota(jnp.int32, (tq, SWA_NK), 0)
        kcol = lax.broadcasted_iota(jnp.int32, (tq, SWA_NK), 1)
        lo = lax.shift_right_logical(qrow, 6) * CHUNK
        mask = (kcol >= lo) & (kcol < lo + WINDOW + CHUNK) & (kcol < WINDOW + tq)
        if prev_valid is not None:
            mask = mask & jnp.logical_or(kcol >= WINDOW, prev_valid)
        bias_scr[...] = jnp.where(mask, 0.0, -jnp.inf)

    state = {}

    def head(kh, g):
        if g == 0:
            state["q4"] = (qkv_ref[r0:r0 + tq, kh * SWA_GW:(kh + 1) * SWA_GW]
                           * (SWA_HD ** -0.5 * LOG2E)).astype(BF16)
            state["acc"] = None
            state["linv"] = []
        sg = _dot(state["q4"], kblk[kh, :, g * SWA_NK:(g + 1) * SWA_NK]) + bias_scr[...]
        sink = sink_ref[kh * SWA_GROUP + g] * LOG2E
        m = jnp.maximum(jnp.max(sg, axis=-1, keepdims=True), sink)
        p = jnp.exp2(sg - m)
        state["linv"].append(1.0 / (jnp.sum(p, axis=-1, keepdims=True) + jnp.exp2(sink - m)))
        part = _dot(p.astype(BF16), vblk[kh, g * SWA_NK:(g + 1) * SWA_NK, :])
        state["acc"] = part if state["acc"] is None else state["acc"] + part
        if g == SWA_GROUP - 1:
            linv = state["linv"]
            lane = lax.broadcasted_iota(jnp.int32, (tq, SWA_GW), 1)
            norm = jnp.where(lane < SWA_HD, linv[0],
                             jnp.where(lane < 2 * SWA_HD, linv[1],
                                       jnp.where(lane < 3 * SWA_HD, linv[2], linv[3])))
            o_ref[r0:r0 + tq, kh * SWA_GW:(kh + 1) * SWA_GW] = (state["acc"] * norm).astype(BF16)

    return [prologue] + [functools.partial(head, kh, g) for kh in range(SWA_KVH) for g in range(SWA_GROUP)]


def _swa_kernel(sink_ref, qkv_ref, kprev_ref, vprev_ref, o_ref, kblk, vblk, bias_scr, *, tq, prev_always_valid):
    t = pl.program_id(1)

    @pl.when(jnp.logical_and(pl.program_id(0) == 0, t == 0))
    def _():
        kblk[...] = jnp.zeros(kblk.shape, BF16)
        vblk[...] = jnp.zeros(vblk.shape, BF16)

    _attend(qkv_ref, 0, tq, lambda: kprev_ref[...], lambda: vprev_ref[...],
            None if prev_always_valid else t > 0, sink_ref, kblk, vblk, bias_scr, o_ref)


def _swa_attention(qkv, k_prev, v_prev, sinks, batch, t_len, prev_from_qkv):
    tq = min(t_len, WINDOW)
    nt = t_len // tq
    if prev_from_qkv:
        assert tq == WINDOW
        kspec = pl.BlockSpec((WINDOW, SWA_DKV),
                             lambda b, t: (jnp.maximum(b * nt + t - 1, 0), SWA_DQ // SWA_DKV))
        vspec = pl.BlockSpec((WINDOW, SWA_DKV),
                             lambda b, t: (jnp.maximum(b * nt + t - 1, 0), SWA_DQ // SWA_DKV + 1))
        k_prev = v_prev = qkv
    else:
        kspec = pl.BlockSpec((WINDOW, SWA_DKV), lambda b, t: (b, 0))
        vspec = pl.BlockSpec((WINDOW, SWA_DKV), lambda b, t: (b, 0))
    return pl.pallas_call(
        functools.partial(_swa_kernel, tq=tq, prev_always_valid=not prev_from_qkv),
        grid=(batch, nt),
        in_specs=[
            pl.BlockSpec(memory_space=pltpu.SMEM),
            pl.BlockSpec((tq, SWA_COLS), lambda b, t: (b * nt + t, 0)),
            kspec,
            vspec,
        ],
        out_specs=pl.BlockSpec((tq, SWA_DQ), lambda b, t: (b * nt + t, 0)),
        out_shape=jax.ShapeDtypeStruct((batch * t_len, SWA_DQ), BF16),
        scratch_shapes=[
            pltpu.VMEM((SWA_KVH, SWA_GW, SWA_GROUP * SWA_NK), BF16),
            pltpu.VMEM((SWA_KVH, SWA_GROUP * SWA_NK, SWA_GW), BF16),
            pltpu.VMEM((tq, SWA_NK), F32),
        ],
        compiler_params=_params(("arbitrary", "arbitrary")),
        name="swa_attention",
    )(sinks, qkv, k_prev, v_prev)


FF_CHUNK = 1024
POST_ROW_GROUPS = 2
LAYER_FF_CHUNK = 512


def _post_rows(o, x, g_ref, wo_ref, bo_ref, wu_ref, wd_ref):
    m = o.shape[0]
    ng = POST_ROW_GROUPS if m % (POST_ROW_GROUPS * 128) == 0 else 1
    rows = [slice(i * (m // ng), (i + 1) * (m // ng)) for i in range(ng)]
    a = [_dot(o[r], wo_ref[...]) + bo_ref[...] for r in rows]
    x1 = [x[r] + _rms(a[i], g_ref[1:2, :]) for i, r in enumerate(rows)]
    h = [_rms(x1[i], g_ref[2:3, :]).astype(BF16) for i in range(ng)]
    acc = [None] * ng
    for j in range(D_FF // FF_CHUNK):
        for i in range(ng):
            u = _dot(h[i], wu_ref[:, j * FF_CHUNK:(j + 1) * FF_CHUNK])
            u = jnp.maximum(u, 0.0)
            u = (u * u).astype(BF16)
            part = _dot(u, wd_ref[j * FF_CHUNK:(j + 1) * FF_CHUNK, :])
            acc[i] = part if acc[i] is None else acc[i] + part
    y = [x1[i] + _rms(acc[i], g_ref[3:4, :]) for i in range(ng)]
    return y[0] if ng == 1 else jnp.concatenate(y, axis=0)


def _post_kernel(*refs, with_next):
    if with_next:
        o_ref, x_ref, g_ref, wo_ref, bo_ref, wu_ref, wd_ref, gn_ref, wn_ref, bn_ref, y_ref, p_ref = refs
    else:
        o_ref, x_ref, g_ref, wo_ref, bo_ref, wu_ref, wd_ref, y_ref = refs
    y = _post_rows(o_ref[...], x_ref[...], g_ref, wo_ref, bo_ref, wu_ref, wd_ref)
    y_ref[...] = y
    if with_next:
        hn = _rms(y, gn_ref[0:1, :]).astype(BF16)
        p_ref[...] = _dot(hn, wn_ref[...]) + bn_ref[...]


def _swa_layer_kernel(sink_ref, qkv_ref, kprev_ref, vprev_ref, x_ref, g_ref, wo_ref, bo_ref, wu_ref, wd_ref,
                      y_ref, kblk, vblk, bias_scr, o_scr, a_scr, *, n_tiles, tile, blocks_per_seq):
    s = pl.program_id(0)

    @pl.when(s == 0)
    def _():
        kblk[...] = jnp.zeros(kblk.shape, BF16)
        vblk[...] = jnp.zeros(vblk.shape, BF16)
        o_scr[...] = jnp.zeros(o_scr.shape, BF16)

    a_tile = jnp.minimum(s, n_tiles - 1)
    n_sub = tile // WINDOW

    pieces = []
    for j in range(n_sub):
        r0 = j * WINDOW
        if j == 0:
            k_prev, v_prev = (lambda: kprev_ref[...]), (lambda: vprev_ref[...])
            prev_valid = lax.rem(a_tile * n_sub, blocks_per_seq) != 0
        else:
            k_prev = functools.partial(lambda r: qkv_ref[r - WINDOW:r, SWA_DQ:SWA_DQ + SWA_DKV], r0)
            v_prev = functools.partial(lambda r: qkv_ref[r - WINDOW:r, SWA_DQ + SWA_DKV:], r0)
            prev_valid = None
        pieces += _attend_pieces(qkv_ref, r0, WINDOW, k_prev, v_prev, prev_valid,
                                 sink_ref, kblk, vblk, bias_scr, a_scr)

    a = _dot(o_scr[...], wo_ref[...]) + bo_ref[...]
    x1 = x_ref[...] + _rms(a, g_ref[1:2, :])
    h = _rms(x1, g_ref[2:3, :]).astype(BF16)

    n_ff = D_FF // LAYER_FF_CHUNK
    st = {"acc": None}

    def up(j):
        v = jnp.maximum(_dot(h, wu_ref[:, j * LAYER_FF_CHUNK:(j + 1) * LAYER_FF_CHUNK]), 0.0)
        st["u"] = (v * v).astype(BF16)

    def down(j):
        part = _dot(st["u"], wd_ref[j * LAYER_FF_CHUNK:(j + 1) * LAYER_FF_CHUNK, :])
        st["acc"] = part if st["acc"] is None else st["acc"] + part

    ff = []
    for j in range(n_ff):
        ff += [functools.partial(up, j), functools.partial(down, j)]
    n_pieces = len(pieces)
    for i, piece in enumerate(ff):
        while n_pieces - len(pieces) < n_pieces * (i + 1) // len(ff):
            pieces.pop(0)()
        piece()
    assert not pieces
    y_ref[...] = x1 + _rms(st["acc"], g_ref[3:4, :])
    o_scr[...] = a_scr[...]


def _swa_layer(qkv, x, sinks, g4, wo, bo, wu, wd, t_len, tile):
    n, d = x.shape
    assert t_len % tile == 0 and tile % WINDOW == 0
    n_tiles = n // tile
    per = tile // WINDOW
    a_idx = lambda s: jnp.minimum(s, n_tiles - 1)
    p_idx = lambda s: jnp.maximum(s - 1, 0)
    return pl.pallas_call(
        functools.partial(_swa_layer_kernel, n_tiles=n_tiles, tile=tile, blocks_per_seq=t_len // WINDOW),
        grid=(n_tiles + 1,),
        in_specs=[
            pl.BlockSpec(memory_space=pltpu.SMEM),
            pl.BlockSpec((tile, SWA_COLS), lambda s: (a_idx(s), 0)),
            pl.BlockSpec((WINDOW, SWA_DKV), lambda s: (jnp.maximum(a_idx(s) * per - 1, 0), SWA_DQ // SWA_DKV)),
            pl.BlockSpec((WINDOW, SWA_DKV), lambda s: (jnp.maximum(a_idx(s) * per - 1, 0), SWA_DQ // SWA_DKV + 1)),
            pl.BlockSpec((tile, d), lambda s: (p_idx(s), 0)),
            _resident((4, d)),
            _resident((d, d)),
            _resident((1, d)),
            _resident((d, D_FF)),
            _resident((D_FF, d)),
        ],
        out_specs=pl.BlockSpec((tile, d), lambda s: (p_idx(s), 0)),
        out_shape=jax.ShapeDtypeStruct((n, d), F32),
        scratch_shapes=[
            pltpu.VMEM((SWA_KVH, SWA_GW, SWA_GROUP * SWA_NK), BF16),
            pltpu.VMEM((SWA_KVH, SWA_GROUP * SWA_NK, SWA_GW), BF16),
            pltpu.VMEM((WINDOW, SWA_NK), F32),
            pltpu.VMEM((tile, SWA_DQ), BF16),
            pltpu.VMEM((tile, SWA_DQ), BF16),
        ],
        compiler_params=_params(("arbitrary",)),
        name="swa_layer",
    )(sinks, qkv, qkv, qkv, x, g4, wo, bo, wu, wd)


def _post_block(o, x, g4, wo, bo, wu, wd, tm, nxt=None):
    n, d = x.shape
    rows = lambda i: (i, 0)
    in_specs = [
        pl.BlockSpec((tm, d), rows),
        pl.BlockSpec((tm, d), rows),
        _resident((4, d)),
        _resident((d, d)),
        _resident((1, d)),
        _resident((d, D_FF)),
        _resident((D_FF, d)),
    ]
    args = [o, x, g4, wo, bo, wu, wd]
    out_specs = [pl.BlockSpec((tm, d), rows)]
    out_shape = [jax.ShapeDtypeStruct((n, d), F32)]
    if nxt is not None:
        gn, wn, bn = nxt
        cols = wn.shape[1]
        in_specs += [_resident((4, d)), _resident((d, cols)), _resident((1, cols))]
        args += [gn, wn, bn]
        out_specs.append(pl.BlockSpec((tm, cols), rows))
        out_shape.append(jax.ShapeDtypeStruct((n, cols), F32))
    outs = pl.pallas_call(
        functools.partial(_post_kernel, with_next=nxt is not None),
        grid=(n // tm,),
        in_specs=in_specs,
        out_specs=out_specs,
        out_shape=out_shape,
        compiler_params=_params(("arbitrary",)),
        name="post_block",
    )(*args)
    return outs if nxt is not None else outs[0]


def _trunk(x, gla_s0, swa_k0, swa_v0, w):
    batch, t_len, d = x.shape
    n = batch * t_len
    tm = min(ROW_TILE, n)
    assert n % tm == 0 and t_len % CHUNK == 0
    xf = x.reshape(n, d)
    og, s_fin = _gla_layer(xf, w["norm_g"][0], w["gla_w_ext"], w["gla_wb"], w["gla_bg"], w["gla_gon"],
                           gla_s0, batch, t_len)
    x1, qkv = _post_block(og, xf, w["norm_g"][0], w["gla_wo"], w["zero_bias"], w["wu"][0], w["wd"][0], tm,
                          nxt=(w["norm_g"][1], w["swa_wi"], w["swa_bi"]))
    if swa_k0 is None and t_len % SWA_LAYER_TILE == 0:
        y = _swa_layer(qkv, x1, w["sinks"], w["norm_g"][1], w["swa_wo"], w["swa_bo"], w["wu"][1], w["wd"][1],
                       t_len, SWA_LAYER_TILE)
    else:
        if swa_k0 is None:
            oa = _swa_attention(qkv, None, None, w["sinks"], batch, t_len, True)
        else:
            kp = swa_k0.reshape(batch * WINDOW, SWA_DKV)
            vp = swa_v0.reshape(batch * WINDOW, SWA_DKV)
            oa = _swa_attention(qkv, kp, vp, w["sinks"], batch, t_len, False)
        y = _post_block(oa, x1, w["norm_g"][1], w["swa_wo"], w["swa_bo"], w["wu"][1], w["wd"][1], tm)
    keep = min(WINDOW, t_len) if swa_k0 is None else swa_k0.shape[1]
    new = min(keep, t_len)
    tail = qkv.reshape(batch, t_len, SWA_COLS)[:, t_len - new:, :]
    k_new = tail[:, :, SWA_DQ:SWA_DQ + SWA_DKV].reshape(batch, new, SWA_KVH, SWA_HD)
    v_new = tail[:, :, SWA_DQ + SWA_DKV:].reshape(batch, new, SWA_KVH, SWA_HD)
    if swa_k0 is None:
        k_cache, v_cache = k_new, v_new
    else:
        k_cache = jnp.concatenate([swa_k0[:, new:], k_new], axis=1)
        v_cache = jnp.concatenate([swa_v0[:, new:], v_new], axis=1)
    return y.reshape(batch, t_len, d), s_fin[None], k_cache[None], v_cache[None]


def kernel(x_prompt, x_sample, state_gla, cache_swa_k, cache_swa_v, gla_w_in, gla_w_gate_a, gla_w_gate_b, gla_b_gate, gla_g_onorm, gla_w_out, swa_w_in, swa_b_in, swa_sinks, swa_w_out, swa_b_out, norm_g, mlp_w_up, mlp_w_down):
    d = D_MODEL
    pad_a = jnp.zeros((d, GLA_GATE_PAD - GLA_GATE_RANK), F32)
    pad_b = jnp.zeros((GLA_GATE_PAD - GLA_GATE_RANK, GLA_DQK), F32)
    w = {
        "norm_g": norm_g,
        "gla_w_ext": jnp.concatenate([gla_w_in[0], gla_w_gate_a[0], pad_a], axis=1).astype(BF16),
        "gla_wb": jnp.concatenate([gla_w_gate_b[0], pad_b], axis=0).astype(BF16),
        "gla_bg": gla_b_gate[0].reshape(1, GLA_DQK),
        "gla_gon": gla_g_onorm[0].reshape(1, GLA_DVT),
        "gla_wo": gla_w_out[0].astype(BF16),
        "zero_bias": jnp.zeros((1, d), F32),
        "swa_wi": swa_w_in[0].astype(BF16),
        "swa_bi": swa_b_in[0].reshape(1, SWA_COLS),
        "sinks": swa_sinks[0],
        "swa_wo": swa_w_out[0].astype(BF16),
        "swa_bo": swa_b_out[0].reshape(1, d),
        "wu": mlp_w_up.astype(BF16),
        "wd": mlp_w_down.astype(BF16),
    }
    y_p, s_p, k_p, v_p = _trunk(x_prompt, None, None, None, w)
    y_s, s_s, k_s, v_s = _trunk(x_sample, state_gla[0], cache_swa_k[0], cache_swa_v[0], w)
    return (y_p, y_s, s_p, s_s, k_p, v_p, k_s, v_s)
```

```python
import functools
import math

import numpy as np
import jax
import jax.numpy as jnp
from jax import lax
from jax.experimental import pallas as pl
from jax.experimental.pallas import tpu as pltpu

F32 = jnp.float32
BF16 = jnp.bfloat16

D_MODEL = 1024
D_FF = 4 * D_MODEL
EPS = 1e-6
CHUNK = 64
LOG2E = math.log2(math.e)
GLA_HEADS = 4
GLA_DK = 128
GLA_DV = 256
GLA_DQK = GLA_HEADS * GLA_DK
GLA_DVT = GLA_HEADS * GLA_DV
GLA_GATE_RANK = 16
GLA_GATE_NORM = 16.0
GLA_GATE_PAD = 128
GLA_PROJ_COLS = 2 * GLA_DQK + 2 * GLA_DVT + GLA_GATE_PAD
GLA_MAX_CHUNK = 256
GLA_PROJ_TILE = 256
GLA_SAFE_DECAY = 60.0
SWA_HD = 64
SWA_QH = 16
SWA_KVH = 4
SWA_GROUP = 4
SWA_DQ = SWA_QH * SWA_HD
SWA_DKV = SWA_KVH * SWA_HD
SWA_COLS = SWA_DQ + 2 * SWA_DKV
SWA_GW = SWA_GROUP * SWA_HD
WINDOW = 128
SWA_NK = 2 * WINDOW
ROW_TILE = 512
SWA_LAYER_TILE = 512
VMEM_LIMIT_BYTES = 60 * 1024 * 1024


def _dot(a, b):
    return jnp.dot(a, b, preferred_element_type=F32)


def _dot_nt(a, b):
    return lax.dot_general(a, b, (((1,), (1,)), ((), ())), preferred_element_type=F32)


def _dot_tn(a, b):
    return lax.dot_general(a, b, (((0,), (0,)), ((), ())), preferred_element_type=F32)


def _rms(x, g):
    return x * lax.rsqrt(jnp.mean(x * x, axis=-1, keepdims=True) + EPS) * g


def _split_bf16(x):
    hi = x.astype(BF16)
    lo = (x - hi.astype(F32)).astype(BF16)
    return hi, lo


def _params(semantics):
    return pltpu.CompilerParams(dimension_semantics=semantics, vmem_limit_bytes=VMEM_LIMIT_BYTES)


def _resident(shape):
    zeros = (0,) * len(shape)
    return pl.BlockSpec(shape, lambda *_: zeros, pipeline_mode=pl.Buffered(1))


def _gla_level_matrices(c):
    levels = []
    s = c
    while s >= 2:
        levels.append(s)
        s //= 2
    t = np.arange(c)[None, :]
    r = np.arange(c)[:, None]
    mats = []
    for s in levels:
        mid = (r // s) * s + s // 2
        mq = (r >= mid) & (t >= mid) & (t <= r)
        mk = (r < mid) & (t > r) & (t <= mid - 1)
        mats.append(mq)
        mats.append(mk)
    return np.stack(mats).astype(np.float32), levels


def _gla_kernel(*refs, c, levels, has_s0, nt):
    if has_s0:
        (x_ref, g_ref, w_ref, wb_ref, bg_ref, gon_ref, tri_ref, lev_ref, s0_ref,
         og_ref, s_ref, a_scr, sprev, proj_a, proj_b) = refs
    else:
        (x_ref, g_ref, w_ref, wb_ref, bg_ref, gon_ref, tri_ref, lev_ref,
         og_ref, s_ref, a_scr, sprev, proj_a, proj_b) = refs
        s0_ref = None
    s = pl.program_id(0)
    first = lax.rem(jnp.maximum(s - 1, 0), nt) == 0

    @pl.when(s == 0)
    def _():
        proj_a[...] = jnp.zeros(proj_a.shape, F32)
        proj_b[...] = jnp.zeros(proj_b.shape, F32)

    scale = GLA_DK ** -0.5
    row = lax.broadcasted_iota(jnp.int32, (c, c), 0)
    col = lax.broadcasted_iota(jnp.int32, (c, c), 1)
    nb = min(c, GLA_DK)

    def projection_pieces(dst):
        h = _rms(x_ref[...], g_ref[0:1, :]).astype(BF16)

        def tile(c0, c1):
            dst[:, c0:c1] = _dot(h, w_ref[:, c0:c1])

        edges = list(range(0, GLA_PROJ_COLS, GLA_PROJ_TILE)) + [GLA_PROJ_COLS]
        return [functools.partial(tile, c0, c1) for c0, c1 in zip(edges[:-1], edges[1:])]

    def step(proj, robust, fillers=()):
        fillers = list(fillers)
        n_slots = 13
        per_slot = -(-len(fillers) // n_slots)

        def fill():
            for _ in range(per_slot):
                if fillers:
                    fillers.pop(0)()

        qk = proj[:, :2 * GLA_DQK]
        v_all = proj[:, 2 * GLA_DQK:2 * GLA_DQK + GLA_DVT].astype(BF16)
        r_all = proj[:, 2 * GLA_DQK + GLA_DVT:2 * GLA_DQK + 2 * GLA_DVT]
        ga = proj[:, 2 * GLA_DQK + 2 * GLA_DVT:].astype(BF16)
        gp = _dot(ga, wb_ref[...]) + bg_ref[...]
        fill()
        soft = jnp.log2(1.0 + jnp.exp2(jnp.abs(gp) * (-LOG2E)))
        lf = jnp.minimum(gp, 0.0) * (LOG2E / GLA_GATE_NORM) - soft * (1.0 / GLA_GATE_NORM)
        lf_hi, lf_lo = _split_bf16(lf)
        fill()
        tri = tri_ref[...]
        b = _dot(tri, lf_hi) + _dot(tri, lf_lo)
        b_last = b[c - 1:c, :]
        fill()

        if robust:
            for hd in range(GLA_HEADS):
                cs = slice(hd * GLA_DK, (hd + 1) * GLA_DK)
                ks = slice(GLA_DQK + hd * GLA_DK, GLA_DQK + (hd + 1) * GLA_DK)
                a_scr[hd] = jnp.where(row == col,
                                      _dot_nt((qk[:, cs] * scale).astype(BF16), qk[:, ks].astype(BF16)), 0.0)
            for li, s in enumerate(levels):
                mq = lev_ref[2 * li]
                mk = lev_ref[2 * li + 1]
                dq = _dot(mq, lf_hi) + _dot(mq, lf_lo)
                dk = _dot(mk, lf_hi) + _dot(mk, lf_lo)
                half = s // 2
                sh = s.bit_length() - 1
                same = lax.shift_right_logical(row, sh) == lax.shift_right_logical(col, sh)
                mask = same & ((row & (s - 1)) >= half) & ((col & (s - 1)) < half)
                for hd in range(GLA_HEADS):
                    cs = slice(hd * GLA_DK, (hd + 1) * GLA_DK)
                    ks = slice(GLA_DQK + hd * GLA_DK, GLA_DQK + (hd + 1) * GLA_DK)
                    qq = (qk[:, cs] * scale * jnp.exp2(dq[:, cs])).astype(BF16)
                    kk = (qk[:, ks] * jnp.exp2(dk[:, cs])).astype(BF16)
                    a_scr[hd] = a_scr[hd] + jnp.where(mask, _dot_nt(qq, kk), 0.0)

        heads = range(GLA_HEADS)
        cs = [slice(hd * GLA_DK, (hd + 1) * GLA_DK) for hd in heads]
        ks = [slice(GLA_DQK + hd * GLA_DK, GLA_DQK + (hd + 1) * GLA_DK) for hd in heads]
        vs = [slice(hd * GLA_DV, (hd + 1) * GLA_DV) for hd in heads]
        qe = [(qk[:, cs[hd]] * scale * jnp.exp2(b[:, cs[hd]])).astype(BF16) for hd in heads]
        fill()
        if robust:
            s_old = [sprev[hd] for hd in heads]
            a = [a_scr[hd].astype(BF16) for hd in heads]
        else:
            init = [s0_ref[0, hd] if has_s0 else jnp.zeros((GLA_DK, GLA_DV), F32) for hd in heads]
            s_old = [jnp.where(first, init[hd], s_ref[0, hd]) for hd in heads]
            for hd in heads:
                sprev[hd] = s_old[hd]
            ke = [(qk[:, ks[hd]] * jnp.exp2(-b[:, cs[hd]])).astype(BF16) for hd in heads]
            fill()
            a = [jnp.where(row >= col, _dot_nt(qe[hd], ke[hd]), 0.0).astype(BF16) for hd in heads]
        fill()
        o = [_dot(qe[hd], s_old[hd].astype(BF16)) + _dot(a[hd], v_all[:, vs[hd]]) for hd in heads]
        fill()
        if not robust:
            kd = [(qk[:, ks[hd]] * jnp.exp2(b_last[:, cs[hd]] - b[:, cs[hd]])).astype(BF16) for hd in heads]
            fill()
            fill()
            for hd in heads:
                dec = jnp.exp2(jnp.transpose(b[c - nb:c, cs[hd]])[:, nb - 1:nb])
                s_ref[0, hd] = s_old[hd] * dec + _dot_tn(kd[hd], v_all[:, vs[hd]])
        for hd in heads:
            fill()
            on = o[hd] * lax.rsqrt(jnp.mean(o[hd] * o[hd], axis=-1, keepdims=True) + EPS) * gon_ref[:, vs[hd]]
            hr = 0.5 * r_all[:, vs[hd]]
            og_ref[:, vs[hd]] = (on * (hr + hr * jnp.tanh(hr))).astype(BF16)
        while fillers:
            fillers.pop(0)()
        return jnp.min(b)

    def body(src, dst):
        b_min = step(src, False, projection_pieces(dst))

        @pl.when(b_min < -GLA_SAFE_DECAY * LOG2E)
        def _():
            step(src, True)

    @pl.when(lax.rem(s, 2) == 0)
    def _():
        body(proj_b, proj_a)

    @pl.when(lax.rem(s, 2) == 1)
    def _():
        body(proj_a, proj_b)


def _gla_layer(x, g4, w_ext, wb, bg, gon, s0, batch, t_len):
    d = x.shape[1]
    c = min(t_len, GLA_MAX_CHUNK)
    nt = t_len // c
    n_chunks = batch * nt
    lev_np, levels = _gla_level_matrices(c)
    lev = jnp.asarray(lev_np, BF16)
    tri = jnp.asarray(np.tril(np.ones((c, c), np.float32)), BF16)
    has_s0 = s0 is not None
    p_idx = lambda s: jnp.minimum(s, n_chunks - 1)
    t_idx = lambda s: jnp.maximum(s - 1, 0)
    in_specs = [
        pl.BlockSpec((c, d), lambda s: (p_idx(s), 0)),
        _resident((4, d)),
        _resident((d, GLA_PROJ_COLS)),
        _resident((GLA_GATE_PAD, GLA_DQK)),
        _resident((1, GLA_DQK)),
        _resident((1, GLA_DVT)),
        _resident((c, c)),
        _resident((2 * len(levels), c, c)),
    ]
    args = [x, g4, w_ext, wb, bg, gon, tri, lev]
    state_spec = pl.BlockSpec((1, GLA_HEADS, GLA_DK, GLA_DV), lambda s: (t_idx(s) // nt, 0, 0, 0))
    if has_s0:
        in_specs.append(state_spec)
        args.append(s0)
    og, s_fin = pl.pallas_call(
        functools.partial(_gla_kernel, c=c, levels=tuple(levels), has_s0=has_s0, nt=nt),
        grid=(n_chunks + 1,),
        in_specs=in_specs,
        out_specs=[
            pl.BlockSpec((c, GLA_DVT), lambda s: (t_idx(s), 0)),
            state_spec,
        ],
        out_shape=[
            jax.ShapeDtypeStruct((batch * t_len, GLA_DVT), BF16),
            jax.ShapeDtypeStruct((batch, GLA_HEADS, GLA_DK, GLA_DV), F32),
        ],
        scratch_shapes=[
            pltpu.VMEM((GLA_HEADS, c, c), F32),
            pltpu.VMEM((GLA_HEADS, GLA_DK, GLA_DV), F32),
            pltpu.VMEM((c, GLA_PROJ_COLS), F32),
            pltpu.VMEM((c, GLA_PROJ_COLS), F32),
        ],
        compiler_params=_params(("arbitrary",)),
        name="gla_layer",
    )(*args)
    return og, s_fin


def _attend(qkv_ref, r0, tq, k_prev, v_prev, prev_valid, sink_ref, kblk, vblk, bias_scr, o_ref):
    for piece in _attend_pieces(qkv_ref, r0, tq, k_prev, v_prev, prev_valid, sink_ref, kblk, vblk, bias_scr, o_ref):
        piece()


def _attend_pieces(qkv_ref, r0, tq, k_prev, v_prev, prev_valid, sink_ref, kblk, vblk, bias_scr, o_ref):
    def prologue():
        pad = SWA_NK - WINDOW - tq
        k_parts = [k_prev(), qkv_ref[r0:r0 + tq, SWA_DQ:SWA_DQ + SWA_DKV]]
        v_parts = [v_prev(), qkv_ref[r0:r0 + tq, SWA_DQ + SWA_DKV:]]
        if pad:
            k_parts.append(jnp.zeros((pad, SWA_DKV), F32))
            v_parts.append(jnp.zeros((pad, SWA_DKV), F32))
        kt = jnp.concatenate(k_parts, axis=0).T.astype(BF16)
        vc = jnp.concatenate(v_parts, axis=0).astype(BF16)
        for kh in range(SWA_KVH):
            for g in range(SWA_GROUP):
                kblk[kh, g * SWA_HD:(g + 1) * SWA_HD, g * SWA_NK:(g + 1) * SWA_NK] = kt[kh * SWA_HD:(kh + 1) * SWA_HD, :]
                vblk[kh, g * SWA_NK:(g + 1) * SWA_NK, g * SWA_HD:(g + 1) * SWA_HD] = vc[:, kh * SWA_HD:(kh + 1) * SWA_HD]
        qrow = lax.broadcasted_iota(jnp.int32, (tq, SWA_NK), 0)
        kcol = lax.broadcasted_iota(jnp.int32, (tq, SWA_NK), 1)
        lo = lax.shift_right_logical(qrow, 6) * CHUNK
        mask = (kcol >= lo) & (kcol < lo + WINDOW + CHUNK) & (kcol < WINDOW + tq)
        if prev_valid is not None:
            mask = mask & jnp.logical_or(kcol >= WINDOW, prev_valid)
        bias_scr[...] = jnp.where(mask, 0.0, -jnp.inf)

    state = {}

    def head(kh, g):
        if g == 0:
            state["q4"] = (qkv_ref[r0:r0 + tq, kh * SWA_GW:(kh + 1) * SWA_GW]
                           * (SWA_HD ** -0.5 * LOG2E)).astype(BF16)
            state["acc"] = None
            state["linv"] = []
        sg = _dot(state["q4"], kblk[kh, :, g * SWA_NK:(g + 1) * SWA_NK]) + bias_scr[...]
        sink = sink_ref[kh * SWA_GROUP + g] * LOG2E
        m = jnp.maximum(jnp.max(sg, axis=-1, keepdims=True), sink)
        p = jnp.exp2(sg - m)
        state["linv"].append(1.0 / (jnp.sum(p, axis=-1, keepdims=True) + jnp.exp2(sink - m)))
        part = _dot(p.astype(BF16), vblk[kh, g * SWA_NK:(g + 1) * SWA_NK, :])
        state["acc"] = part if state["acc"] is None else state["acc"] + part
        if g == SWA_GROUP - 1:
            linv = state["linv"]
            lane = lax.broadcasted_iota(jnp.int32, (tq, SWA_GW), 1)
            norm = jnp.where(lane < SWA_HD, linv[0],
                             jnp.where(lane < 2 * SWA_HD, linv[1],
                                       jnp.where(lane < 3 * SWA_HD, linv[2], linv[3])))
            o_ref[r0:r0 + tq, kh * SWA_GW:(kh + 1) * SWA_GW] = (state["acc"] * norm).astype(BF16)

    return [prologue] + [functools.partial(head, kh, g) for kh in range(SWA_KVH) for g in range(SWA_GROUP)]


def _swa_kernel(sink_ref, qkv_ref, kprev_ref, vprev_ref, o_ref, kblk, vblk, bias_scr, *, tq, prev_always_valid):
    t = pl.program_id(1)

    @pl.when(jnp.logical_and(pl.program_id(0) == 0, t == 0))
    def _():
        kblk[...] = jnp.zeros(kblk.shape, BF16)
        vblk[...] = jnp.zeros(vblk.shape, BF16)

    _attend(qkv_ref, 0, tq, lambda: kprev_ref[...], lambda: vprev_ref[...],
            None if prev_always_valid else t > 0, sink_ref, kblk, vblk, bias_scr, o_ref)


def _swa_attention(qkv, k_prev, v_prev, sinks, batch, t_len, prev_from_qkv):
    tq = min(t_len, WINDOW)
    nt = t_len // tq
    if prev_from_qkv:
        assert tq == WINDOW
        kspec = pl.BlockSpec((WINDOW, SWA_DKV),
                             lambda b, t: (jnp.maximum(b * nt + t - 1, 0), SWA_DQ // SWA_DKV))
        vspec = pl.BlockSpec((WINDOW, SWA_DKV),
                             lambda b, t: (jnp.maximum(b * nt + t - 1, 0), SWA_DQ // SWA_DKV + 1))
        k_prev = v_prev = qkv
    else:
        kspec = pl.BlockSpec((WINDOW, SWA_DKV), lambda b, t: (b, 0))
        vspec = pl.BlockSpec((WINDOW, SWA_DKV), lambda b, t: (b, 0))
    return pl.pallas_call(
        functools.partial(_swa_kernel, tq=tq, prev_always_valid=not prev_from_qkv),
        grid=(batch, nt),
        in_specs=[
            pl.BlockSpec(memory_space=pltpu.SMEM),
            pl.BlockSpec((tq, SWA_COLS), lambda b, t: (b * nt + t, 0)),
            kspec,
            vspec,
        ],
        out_specs=pl.BlockSpec((tq, SWA_DQ), lambda b, t: (b * nt + t, 0)),
        out_shape=jax.ShapeDtypeStruct((batch * t_len, SWA_DQ), BF16),
        scratch_shapes=[
            pltpu.VMEM((SWA_KVH, SWA_GW, SWA_GROUP * SWA_NK), BF16),
            pltpu.VMEM((SWA_KVH, SWA_GROUP * SWA_NK, SWA_GW), BF16),
            pltpu.VMEM((tq, SWA_NK), F32),
        ],
        compiler_params=_params(("arbitrary", "arbitrary")),
        name="swa_attention",
    )(sinks, qkv, k_prev, v_prev)


FF_CHUNK = 1024
POST_ROW_GROUPS = 2
LAYER_FF_CHUNK = 512


def _post_rows(o, x, g_ref, wo_ref, bo_ref, wu_ref, wd_ref):
    m = o.shape[0]
    ng = POST_ROW_GROUPS if m % (POST_ROW_GROUPS * 128) == 0 else 1
    rows = [slice(i * (m // ng), (i + 1) * (m // ng)) for i in range(ng)]
    a = [_dot(o[r], wo_ref[...]) + bo_ref[...] for r in rows]
    x1 = [x[r] + _rms(a[i], g_ref[1:2, :]) for i, r in enumerate(rows)]
    h = [_rms(x1[i], g_ref[2:3, :]).astype(BF16) for i in range(ng)]
    acc = [None] * ng
    for j in range(D_FF // FF_CHUNK):
        for i in range(ng):
            u = _dot(h[i], wu_ref[:, j * FF_CHUNK:(j + 1) * FF_CHUNK])
            u = jnp.maximum(u, 0.0)
            u = (u * u).astype(BF16)
            part = _dot(u, wd_ref[j * FF_CHUNK:(j + 1) * FF_CHUNK, :])
            acc[i] = part if acc[i] is None else acc[i] + part
    y = [x1[i] + _rms(acc[i], g_ref[3:4, :]) for i in range(ng)]
    return y[0] if ng == 1 else jnp.concatenate(y, axis=0)


def _post_kernel(*refs, with_next, with_tail):
    kc_ref = vc_ref = None
    if with_tail:
        (o_ref, x_ref, g_ref, wo_ref, bo_ref, wu_ref, wd_ref, gn_ref, wn_ref, bn_ref,
         y_ref, p_ref, kc_ref, vc_ref) = refs
    elif with_next:
        o_ref, x_ref, g_ref, wo_ref, bo_ref, wu_ref, wd_ref, gn_ref, wn_ref, bn_ref, y_ref, p_ref = refs
    else:
        o_ref, x_ref, g_ref, wo_ref, bo_ref, wu_ref, wd_ref, y_ref = refs
    y = _post_rows(o_ref[...], x_ref[...], g_ref, wo_ref, bo_ref, wu_ref, wd_ref)
    y_ref[...] = y
    if with_next:
        hn = _rms(y, gn_ref[0:1, :]).astype(BF16)
        p = _dot(hn, wn_ref[...]) + bn_ref[...]
        p_ref[...] = p
        if with_tail:
            tm = p.shape[0]
            kc_ref[...] = p[tm - WINDOW:, SWA_DQ:SWA_DQ + SWA_DKV]
            vc_ref[...] = p[tm - WINDOW:, SWA_DQ + SWA_DKV:]


def _swa_layer_kernel(sink_ref, qkv_ref, kprev_ref, vprev_ref, x_ref, g_ref, wo_ref, bo_ref, wu_ref, wd_ref,
                      y_ref, kblk, vblk, bias_scr, o_scr, a_scr, *, n_tiles, tile, blocks_per_seq):
    s = pl.program_id(0)

    @pl.when(s == 0)
    def _():
        kblk[...] = jnp.zeros(kblk.shape, BF16)
        vblk[...] = jnp.zeros(vblk.shape, BF16)
        o_scr[...] = jnp.zeros(o_scr.shape, BF16)

    a_tile = jnp.minimum(s, n_tiles - 1)
    n_sub = tile // WINDOW

    pieces = []
    for j in range(n_sub):
        r0 = j * WINDOW
        if j == 0:
            k_prev, v_prev = (lambda: kprev_ref[...]), (lambda: vprev_ref[...])
            prev_valid = lax.rem(a_tile * n_sub, blocks_per_seq) != 0
        else:
            k_prev = functools.partial(lambda r: qkv_ref[r - WINDOW:r, SWA_DQ:SWA_DQ + SWA_DKV], r0)
            v_prev = functools.partial(lambda r: qkv_ref[r - WINDOW:r, SWA_DQ + SWA_DKV:], r0)
            prev_valid = None
        pieces += _attend_pieces(qkv_ref, r0, WINDOW, k_prev, v_prev, prev_valid,
                                 sink_ref, kblk, vblk, bias_scr, a_scr)

    a = _dot(o_scr[...], wo_ref[...]) + bo_ref[...]
    x1 = x_ref[...] + _rms(a, g_ref[1:2, :])
    h = _rms(x1, g_ref[2:3, :]).astype(BF16)

    n_ff = D_FF // LAYER_FF_CHUNK
    st = {"acc": None}

    def up(j):
        v = jnp.maximum(_dot(h, wu_ref[:, j * LAYER_FF_CHUNK:(j + 1) * LAYER_FF_CHUNK]), 0.0)
        st["u"] = (v * v).astype(BF16)

    def down(j):
        part = _dot(st["u"], wd_ref[j * LAYER_FF_CHUNK:(j + 1) * LAYER_FF_CHUNK, :])
        st["acc"] = part if st["acc"] is None else st["acc"] + part

    ff = []
    for j in range(n_ff):
        ff += [functools.partial(up, j), functools.partial(down, j)]
    n_pieces = len(pieces)
    for i, piece in enumerate(ff):
        while n_pieces - len(pieces) < n_pieces * (i + 1) // len(ff):
            pieces.pop(0)()
        piece()
    assert not pieces
    y_ref[...] = x1 + _rms(st["acc"], g_ref[3:4, :])
    o_scr[...] = a_scr[...]


def _swa_layer(qkv, x, sinks, g4, wo, bo, wu, wd, t_len, tile):
    n, d = x.shape
    assert t_len % tile == 0 and tile % WINDOW == 0
    n_tiles = n // tile
    per = tile // WINDOW
    a_idx = lambda s: jnp.minimum(s, n_tiles - 1)
    p_idx = lambda s: jnp.maximum(s - 1, 0)
    return pl.pallas_call(
        functools.partial(_swa_layer_kernel, n_tiles=n_tiles, tile=tile, blocks_per_seq=t_len // WINDOW),
        grid=(n_tiles + 1,),
        in_specs=[
            pl.BlockSpec(memory_space=pltpu.SMEM),
            pl.BlockSpec((tile, SWA_COLS), lambda s: (a_idx(s), 0)),
            pl.BlockSpec((WINDOW, SWA_DKV), lambda s: (jnp.maximum(a_idx(s) * per - 1, 0), SWA_DQ // SWA_DKV)),
            pl.BlockSpec((WINDOW, SWA_DKV), lambda s: (jnp.maximum(a_idx(s) * per - 1, 0), SWA_DQ // SWA_DKV + 1)),
            pl.BlockSpec((tile, d), lambda s: (p_idx(s), 0)),
            _resident((4, d)),
            _resident((d, d)),
            _resident((1, d)),
            _resident((d, D_FF)),
            _resident((D_FF, d)),
        ],
        out_specs=pl.BlockSpec((tile, d), lambda s: (p_idx(s), 0)),
        out_shape=jax.ShapeDtypeStruct((n, d), F32),
        scratch_shapes=[
            pltpu.VMEM((SWA_KVH, SWA_GW, SWA_GROUP * SWA_NK), BF16),
            pltpu.VMEM((SWA_KVH, SWA_GROUP * SWA_NK, SWA_GW), BF16),
            pltpu.VMEM((WINDOW, SWA_NK), F32),
            pltpu.VMEM((tile, SWA_DQ), BF16),
            pltpu.VMEM((tile, SWA_DQ), BF16),
        ],
        compiler_params=_params(("arbitrary",)),
        name="swa_layer",
    )(sinks, qkv, qkv, qkv, x, g4, wo, bo, wu, wd)


def _post_block(o, x, g4, wo, bo, wu, wd, tm, nxt=None, seq_tiles=None):
    n, d = x.shape
    rows = lambda i: (i, 0)
    in_specs = [
        pl.BlockSpec((tm, d), rows),
        pl.BlockSpec((tm, d), rows),
        _resident((4, d)),
        _resident((d, d)),
        _resident((1, d)),
        _resident((d, D_FF)),
        _resident((D_FF, d)),
    ]
    args = [o, x, g4, wo, bo, wu, wd]
    out_specs = [pl.BlockSpec((tm, d), rows)]
    out_shape = [jax.ShapeDtypeStruct((n, d), F32)]
    if nxt is not None:
        gn, wn, bn = nxt
        cols = wn.shape[1]
        in_specs += [_resident((4, d)), _resident((d, cols)), _resident((1, cols))]
        args += [gn, wn, bn]
        out_specs.append(pl.BlockSpec((tm, cols), rows))
        out_shape.append(jax.ShapeDtypeStruct((n, cols), F32))
    if seq_tiles is not None:
        assert nxt is not None and tm >= WINDOW and (n // tm) % seq_tiles == 0
        n_seq = n // tm // seq_tiles
        for _ in range(2):
            out_specs.append(pl.BlockSpec((WINDOW, SWA_DKV), lambda i: (i // seq_tiles, 0)))
            out_shape.append(jax.ShapeDtypeStruct((n_seq * WINDOW, SWA_DKV), F32))
    outs = pl.pallas_call(
        functools.partial(_post_kernel, with_next=nxt is not None, with_tail=seq_tiles is not None),
        grid=(n // tm,),
        in_specs=in_specs,
        out_specs=out_specs,
        out_shape=out_shape,
        compiler_params=_params(("arbitrary",)),
        name="post_block",
    )(*args)
    return outs if nxt is not None else outs[0]


def _trunk(x, gla_s0, swa_k0, swa_v0, w):
    batch, t_len, d = x.shape
    n = batch * t_len
    tm = min(ROW_TILE, n)
    assert n % tm == 0 and t_len % CHUNK == 0
    xf = x.reshape(n, d)
    og, s_fin = _gla_layer(xf, w["norm_g"][0], w["gla_w_ext"], w["gla_wb"], w["gla_bg"], w["gla_gon"],
                           gla_s0, batch, t_len)
    tail_from_kernel = swa_k0 is None and t_len % tm == 0 and tm >= WINDOW
    outs = _post_block(og, xf, w["norm_g"][0], w["gla_wo"], w["zero_bias"], w["wu"][0], w["wd"][0], tm,
                       nxt=(w["norm_g"][1], w["swa_wi"], w["swa_bi"]),
                       seq_tiles=t_len // tm if tail_from_kernel else None)
    x1, qkv = outs[0], outs[1]
    if swa_k0 is None and t_len % SWA_LAYER_TILE == 0:
        y = _swa_layer(qkv, x1, w["sinks"], w["norm_g"][1], w["swa_wo"], w["swa_bo"], w["wu"][1], w["wd"][1],
                       t_len, SWA_LAYER_TILE)
    else:
        if swa_k0 is None:
            oa = _swa_attention(qkv, None, None, w["sinks"], batch, t_len, True)
        else:
            kp = swa_k0.reshape(batch * WINDOW, SWA_DKV)
            vp = swa_v0.reshape(batch * WINDOW, SWA_DKV)
            oa = _swa_attention(qkv, kp, vp, w["sinks"], batch, t_len, False)
        y = _post_block(oa, x1, w["norm_g"][1], w["swa_wo"], w["swa_bo"], w["wu"][1], w["wd"][1], tm)
    if tail_from_kernel:
        k_cache = outs[2].reshape(batch, WINDOW, SWA_KVH, SWA_HD)
        v_cache = outs[3].reshape(batch, WINDOW, SWA_KVH, SWA_HD)
    else:
        keep = min(WINDOW, t_len) if swa_k0 is None else swa_k0.shape[1]
        new = min(keep, t_len)
        tail = qkv.reshape(batch, t_len, SWA_COLS)[:, t_len - new:, :]
        k_new = tail[:, :, SWA_DQ:SWA_DQ + SWA_DKV].reshape(batch, new, SWA_KVH, SWA_HD)
        v_new = tail[:, :, SWA_DQ + SWA_DKV:].reshape(batch, new, SWA_KVH, SWA_HD)
        if swa_k0 is None:
            k_cache, v_cache = k_new, v_new
        else:
            k_cache = jnp.concatenate([swa_k0[:, new:], k_new], axis=1)
            v_cache = jnp.concatenate([swa_v0[:, new:], v_new], axis=1)
    return y.reshape(batch, t_len, d), s_fin[None], k_cache[None], v_cache[None]


def kernel(x_prompt, x_sample, state_gla, cache_swa_k, cache_swa_v, gla_w_in, gla_w_gate_a, gla_w_gate_b, gla_b_gate, gla_g_onorm, gla_w_out, swa_w_in, swa_b_in, swa_sinks, swa_w_out, swa_b_out, norm_g, mlp_w_up, mlp_w_down):
    d = D_MODEL
    pad_a = jnp.zeros((d, GLA_GATE_PAD - GLA_GATE_RANK), F32)
    pad_b = jnp.zeros((GLA_GATE_PAD - GLA_GATE_RANK, GLA_DQK), F32)
    w = {
        "norm_g": norm_g,
        "gla_w_ext": jnp.concatenate([gla_w_in[0], gla_w_gate_a[0], pad_a], axis=1).astype(BF16),
        "gla_wb": jnp.concatenate([gla_w_gate_b[0], pad_b], axis=0).astype(BF16),
        "gla_bg": gla_b_gate[0].reshape(1, GLA_DQK),
        "gla_gon": gla_g_onorm[0].reshape(1, GLA_DVT),
        "gla_wo": gla_w_out[0].astype(BF16),
        "zero_bias": jnp.zeros((1, d), F32),
        "swa_wi": swa_w_in[0].astype(BF16),
        "swa_bi": swa_b_in[0].reshape(1, SWA_COLS),
        "sinks": swa_sinks[0],
        "swa_wo": swa_w_out[0].astype(BF16),
        "swa_bo": swa_b_out[0].reshape(1, d),
        "wu": mlp_w_up.astype(BF16),
        "wd": mlp_w_down.astype(BF16),
    }
    y_p, s_p, k_p, v_p = _trunk(x_prompt, None, None, None, w)
    y_s, s_s, k_s, v_s = _trunk(x_sample, state_gla[0], cache_swa_k[0], cache_swa_v[0], w)
    return (y_p, y_s, s_p, s_s, k_p, v_p, k_s, v_s)
```

```python
import functools
import math

import numpy as np
import jax
import jax.numpy as jnp
from jax import lax
from jax.experimental import pallas as pl
from jax.experimental.pallas import tpu as pltpu

F32 = jnp.float32
BF16 = jnp.bfloat16

D_MODEL = 1024
D_FF = 4 * D_MODEL
EPS = 1e-6
CHUNK = 64
LOG2E = math.log2(math.e)
GLA_HEADS = 4
GLA_DK = 128
GLA_DV = 256
GLA_DQK = GLA_HEADS * GLA_DK
GLA_DVT = GLA_HEADS * GLA_DV
GLA_GATE_RANK = 16
GLA_GATE_NORM = 16.0
GLA_GATE_PAD = 128
GLA_PROJ_COLS = 2 * GLA_DQK + 2 * GLA_DVT + GLA_GATE_PAD
GLA_MAX_CHUNK = 256
GLA_PROJ_TILE = 256
GLA_CHUNKS_PER_STEP = 2
GLA_SAFE_DECAY = 60.0
SWA_HD = 64
SWA_QH = 16
SWA_KVH = 4
SWA_GROUP = 4
SWA_DQ = SWA_QH * SWA_HD
SWA_DKV = SWA_KVH * SWA_HD
SWA_COLS = SWA_DQ + 2 * SWA_DKV
SWA_GW = SWA_GROUP * SWA_HD
WINDOW = 128
SWA_NK = 2 * WINDOW
ROW_TILE = 512
SWA_LAYER_TILE = 512
VMEM_LIMIT_BYTES = 60 * 1024 * 1024


def _dot(a, b):
    return jnp.dot(a, b, preferred_element_type=F32)


def _dot_nt(a, b):
    return lax.dot_general(a, b, (((1,), (1,)), ((), ())), preferred_element_type=F32)


def _dot_tn(a, b):
    return lax.dot_general(a, b, (((0,), (0,)), ((), ())), preferred_element_type=F32)


def _rms(x, g):
    return x * lax.rsqrt(jnp.mean(x * x, axis=-1, keepdims=True) + EPS) * g


def _split_bf16(x):
    hi = x.astype(BF16)
    lo = (x - hi.astype(F32)).astype(BF16)
    return hi, lo


def _params(semantics):
    return pltpu.CompilerParams(dimension_semantics=semantics, vmem_limit_bytes=VMEM_LIMIT_BYTES)


def _resident(shape):
    zeros = (0,) * len(shape)
    return pl.BlockSpec(shape, lambda *_: zeros, pipeline_mode=pl.Buffered(1))


def _gla_level_matrices(c):
    levels = []
    s = c
    while s >= 2:
        levels.append(s)
        s //= 2
    t = np.arange(c)[None, :]
    r = np.arange(c)[:, None]
    mats = []
    for s in levels:
        mid = (r // s) * s + s // 2
        mq = (r >= mid) & (t >= mid) & (t <= r)
        mk = (r < mid) & (t > r) & (t <= mid - 1)
        mats.append(mq)
        mats.append(mk)
    return np.stack(mats).astype(np.float32), levels


def _gla_kernel(*refs, c, levels, has_s0, nt, cps):
    if has_s0:
        (x_ref, g_ref, w_ref, wb_ref, bg_ref, gon_ref, tri_ref, lev_ref, s0_ref,
         og_ref, s_ref, a_scr, sprev, proj_a, proj_b) = refs
    else:
        (x_ref, g_ref, w_ref, wb_ref, bg_ref, gon_ref, tri_ref, lev_ref,
         og_ref, s_ref, a_scr, sprev, proj_a, proj_b) = refs
        s0_ref = None
    s = pl.program_id(0)
    t0 = jnp.maximum(s - 1, 0) * cps

    @pl.when(s == 0)
    def _():
        proj_a[...] = jnp.zeros(proj_a.shape, F32)
        proj_b[...] = jnp.zeros(proj_b.shape, F32)

    scale = GLA_DK ** -0.5
    row = lax.broadcasted_iota(jnp.int32, (c, c), 0)
    col = lax.broadcasted_iota(jnp.int32, (c, c), 1)
    nb = min(c, GLA_DK)

    def projection_pieces(dst):
        h = _rms(x_ref[...], g_ref[0:1, :]).astype(BF16)

        def tile(c0, c1):
            dst[:, c0:c1] = _dot(h, w_ref[:, c0:c1])

        edges = list(range(0, GLA_PROJ_COLS, GLA_PROJ_TILE)) + [GLA_PROJ_COLS]
        return [functools.partial(tile, c0, c1) for c0, c1 in zip(edges[:-1], edges[1:])]

    def step(proj_ref, j, robust, fillers=()):
        rows = slice(j * c, (j + 1) * c)
        proj = proj_ref.at[rows, :]
        first = lax.rem(t0 + j, nt) == 0
        fillers = list(fillers)
        n_slots = 13
        per_slot = -(-len(fillers) // n_slots)

        def fill():
            for _ in range(per_slot):
                if fillers:
                    fillers.pop(0)()

        qk = proj[:, :2 * GLA_DQK]
        v_all = proj[:, 2 * GLA_DQK:2 * GLA_DQK + GLA_DVT].astype(BF16)
        r_all = proj[:, 2 * GLA_DQK + GLA_DVT:2 * GLA_DQK + 2 * GLA_DVT]
        ga = proj[:, 2 * GLA_DQK + 2 * GLA_DVT:].astype(BF16)
        gp = _dot(ga, wb_ref[...]) + bg_ref[...]
        fill()
        soft = jnp.log2(1.0 + jnp.exp2(jnp.abs(gp) * (-LOG2E)))
        lf = jnp.minimum(gp, 0.0) * (LOG2E / GLA_GATE_NORM) - soft * (1.0 / GLA_GATE_NORM)
        lf_hi, lf_lo = _split_bf16(lf)
        fill()
        tri = tri_ref[...]
        b = _dot(tri, lf_hi) + _dot(tri, lf_lo)
        b_last = b[c - 1:c, :]
        fill()

        if robust:
            for hd in range(GLA_HEADS):
                cs = slice(hd * GLA_DK, (hd + 1) * GLA_DK)
                ks = slice(GLA_DQK + hd * GLA_DK, GLA_DQK + (hd + 1) * GLA_DK)
                a_scr[hd] = jnp.where(row == col,
                                      _dot_nt((qk[:, cs] * scale).astype(BF16), qk[:, ks].astype(BF16)), 0.0)
            for li, s in enumerate(levels):
                mq = lev_ref[2 * li]
                mk = lev_ref[2 * li + 1]
                dq = _dot(mq, lf_hi) + _dot(mq, lf_lo)
                dk = _dot(mk, lf_hi) + _dot(mk, lf_lo)
                half = s // 2
                sh = s.bit_length() - 1
                same = lax.shift_right_logical(row, sh) == lax.shift_right_logical(col, sh)
                mask = same & ((row & (s - 1)) >= half) & ((col & (s - 1)) < half)
                for hd in range(GLA_HEADS):
                    cs = slice(hd * GLA_DK, (hd + 1) * GLA_DK)
                    ks = slice(GLA_DQK + hd * GLA_DK, GLA_DQK + (hd + 1) * GLA_DK)
                    qq = (qk[:, cs] * scale * jnp.exp2(dq[:, cs])).astype(BF16)
                    kk = (qk[:, ks] * jnp.exp2(dk[:, cs])).astype(BF16)
                    a_scr[hd] = a_scr[hd] + jnp.where(mask, _dot_nt(qq, kk), 0.0)

        heads = range(GLA_HEADS)
        cs = [slice(hd * GLA_DK, (hd + 1) * GLA_DK) for hd in heads]
        ks = [slice(GLA_DQK + hd * GLA_DK, GLA_DQK + (hd + 1) * GLA_DK) for hd in heads]
        vs = [slice(hd * GLA_DV, (hd + 1) * GLA_DV) for hd in heads]
        qe = [(qk[:, cs[hd]] * scale * jnp.exp2(b[:, cs[hd]])).astype(BF16) for hd in heads]
        fill()
        if robust:
            s_old = [sprev[j, hd] for hd in heads]
            a = [a_scr[hd].astype(BF16) for hd in heads]
        else:
            init = [s0_ref[0, hd] if has_s0 else jnp.zeros((GLA_DK, GLA_DV), F32) for hd in heads]
            s_old = [jnp.where(first, init[hd], s_ref[0, hd]) for hd in heads]
            for hd in heads:
                sprev[j, hd] = s_old[hd]
            ke = [(qk[:, ks[hd]] * jnp.exp2(-b[:, cs[hd]])).astype(BF16) for hd in heads]
            fill()
            a = [jnp.where(row >= col, _dot_nt(qe[hd], ke[hd]), 0.0).astype(BF16) for hd in heads]
        fill()
        o = [_dot(qe[hd], s_old[hd].astype(BF16)) + _dot(a[hd], v_all[:, vs[hd]]) for hd in heads]
        fill()
        if not robust:
            kd = [(qk[:, ks[hd]] * jnp.exp2(b_last[:, cs[hd]] - b[:, cs[hd]])).astype(BF16) for hd in heads]
            fill()
            fill()
            for hd in heads:
                dec = jnp.exp2(jnp.transpose(b[c - nb:c, cs[hd]])[:, nb - 1:nb])
                s_ref[0, hd] = s_old[hd] * dec + _dot_tn(kd[hd], v_all[:, vs[hd]])
        for hd in heads:
            fill()
            on = o[hd] * lax.rsqrt(jnp.mean(o[hd] * o[hd], axis=-1, keepdims=True) + EPS) * gon_ref[:, vs[hd]]
            hr = 0.5 * r_all[:, vs[hd]]
            og_ref[rows, vs[hd]] = (on * (hr + hr * jnp.tanh(hr))).astype(BF16)
        while fillers:
            fillers.pop(0)()
        return jnp.min(b)

    def body(src, dst):
        pieces = projection_pieces(dst)
        share = -(-len(pieces) // cps)
        b_min = [step(src, j, False, pieces[j * share:(j + 1) * share]) for j in range(cps)]
        for j in range(cps):
            @pl.when(b_min[j] < -GLA_SAFE_DECAY * LOG2E)
            def _(j=j):
                step(src, j, True)

    @pl.when(lax.rem(s, 2) == 0)
    def _():
        body(proj_b, proj_a)

    @pl.when(lax.rem(s, 2) == 1)
    def _():
        body(proj_a, proj_b)


def _gla_layer(x, g4, w_ext, wb, bg, gon, s0, batch, t_len):
    d = x.shape[1]
    c = min(t_len, GLA_MAX_CHUNK)
    nt = t_len // c
    cps = GLA_CHUNKS_PER_STEP if nt % GLA_CHUNKS_PER_STEP == 0 else 1
    n_steps = batch * nt // cps
    lev_np, levels = _gla_level_matrices(c)
    lev = jnp.asarray(lev_np, BF16)
    tri = jnp.asarray(np.tril(np.ones((c, c), np.float32)), BF16)
    has_s0 = s0 is not None
    p_idx = lambda s: jnp.minimum(s, n_steps - 1)
    t_idx = lambda s: jnp.maximum(s - 1, 0)
    in_specs = [
        pl.BlockSpec((cps * c, d), lambda s: (p_idx(s), 0)),
        _resident((4, d)),
        _resident((d, GLA_PROJ_COLS)),
        _resident((GLA_GATE_PAD, GLA_DQK)),
        _resident((1, GLA_DQK)),
        _resident((1, GLA_DVT)),
        _resident((c, c)),
        _resident((2 * len(levels), c, c)),
    ]
    args = [x, g4, w_ext, wb, bg, gon, tri, lev]
    state_spec = pl.BlockSpec((1, GLA_HEADS, GLA_DK, GLA_DV), lambda s: (t_idx(s) * cps // nt, 0, 0, 0))
    if has_s0:
        in_specs.append(state_spec)
        args.append(s0)
    og, s_fin = pl.pallas_call(
        functools.partial(_gla_kernel, c=c, levels=tuple(levels), has_s0=has_s0, nt=nt, cps=cps),
        grid=(n_steps + 1,),
        in_specs=in_specs,
        out_specs=[
            pl.BlockSpec((cps * c, GLA_DVT), lambda s: (t_idx(s), 0)),
            state_spec,
        ],
        out_shape=[
            jax.ShapeDtypeStruct((batch * t_len, GLA_DVT), BF16),
            jax.ShapeDtypeStruct((batch, GLA_HEADS, GLA_DK, GLA_DV), F32),
        ],
        scratch_shapes=[
            pltpu.VMEM((GLA_HEADS, c, c), F32),
            pltpu.VMEM((cps, GLA_HEADS, GLA_DK, GLA_DV), F32),
            pltpu.VMEM((cps * c, GLA_PROJ_COLS), F32),
            pltpu.VMEM((cps * c, GLA_PROJ_COLS), F32),
        ],
        compiler_params=_params(("arbitrary",)),
        name="gla_layer",
    )(*args)
    return og, s_fin


def _attend(qkv_ref, r0, tq, k_prev, v_prev, prev_valid, sink_ref, kblk, vblk, bias_scr, o_ref):
    for piece in _attend_pieces(qkv_ref, r0, tq, k_prev, v_prev, prev_valid, sink_ref, kblk, vblk, bias_scr, o_ref):
        piece()


def _attend_pieces(qkv_ref, r0, tq, k_prev, v_prev, prev_valid, sink_ref, kblk, vblk, bias_scr, o_ref):
    def prologue():
        pad = SWA_NK - WINDOW - tq
        k_parts = [k_prev(), qkv_ref[r0:r0 + tq, SWA_DQ:SWA_DQ + SWA_DKV]]
        v_parts = [v_prev(), qkv_ref[r0:r0 + tq, SWA_DQ + SWA_DKV:]]
        if pad:
            k_parts.append(jnp.zeros((pad, SWA_DKV), F32))
            v_parts.append(jnp.zeros((pad, SWA_DKV), F32))
        kt = jnp.concatenate(k_parts, axis=0).T.astype(BF16)
        vc = jnp.concatenate(v_parts, axis=0).astype(BF16)
        for kh in range(SWA_KVH):
            for g in range(SWA_GROUP):
                kblk[kh, g * SWA_HD:(g + 1) * SWA_HD, g * SWA_NK:(g + 1) * SWA_NK] = kt[kh * SWA_HD:(kh + 1) * SWA_HD, :]
                vblk[kh, g * SWA_NK:(g + 1) * SWA_NK, g * SWA_HD:(g + 1) * SWA_HD] = vc[:, kh * SWA_HD:(kh + 1) * SWA_HD]
        qrow = lax.broadcasted_iota(jnp.int32, (tq, SWA_NK), 0)
        kcol = lax.broadcasted_iota(jnp.int32, (tq, SWA_NK), 1)
        lo = lax.shift_right_logical(qrow, 6) * CHUNK
        mask = (kcol >= lo) & (kcol < lo + WINDOW + CHUNK) & (kcol < WINDOW + tq)
        if prev_valid is not None:
            mask = mask & jnp.logical_or(kcol >= WINDOW, prev_valid)
        bias_scr[...] = jnp.where(mask, 0.0, -jnp.inf)

    state = {}

    def head(kh, g):
        if g == 0:
            state["q4"] = (qkv_ref[r0:r0 + tq, kh * SWA_GW:(kh + 1) * SWA_GW]
                           * (SWA_HD ** -0.5 * LOG2E)).astype(BF16)
            state["acc"] = None
            state["linv"] = []
        sg = _dot(state["q4"], kblk[kh, :, g * SWA_NK:(g + 1) * SWA_NK]) + bias_scr[...]
        sink = sink_ref[kh * SWA_GROUP + g] * LOG2E
        m = jnp.maximum(jnp.max(sg, axis=-1, keepdims=True), sink)
        p = jnp.exp2(sg - m)
        state["linv"].append(1.0 / (jnp.sum(p, axis=-1, keepdims=True) + jnp.exp2(sink - m)))
        part = _dot(p.astype(BF16), vblk[kh, g * SWA_NK:(g + 1) * SWA_NK, :])
        state["acc"] = part if state["acc"] is None else state["acc"] + part
        if g == SWA_GROUP - 1:
            linv = state["linv"]
            lane = lax.broadcasted_iota(jnp.int32, (tq, SWA_GW), 1)
            norm = jnp.where(lane < SWA_HD, linv[0],
                             jnp.where(lane < 2 * SWA_HD, linv[1],
                                       jnp.where(lane < 3 * SWA_HD, linv[2], linv[3])))
            o_ref[r0:r0 + tq, kh * SWA_GW:(kh + 1) * SWA_GW] = (state["acc"] * norm).astype(BF16)

    return [prologue] + [functools.partial(head, kh, g) for kh in range(SWA_KVH) for g in range(SWA_GROUP)]


def _swa_kernel(sink_ref, qkv_ref, kprev_ref, vprev_ref, o_ref, kblk, vblk, bias_scr, *, tq, prev_always_valid):
    t = pl.program_id(1)

    @pl.when(jnp.logical_and(pl.program_id(0) == 0, t == 0))
    def _():
        kblk[...] = jnp.zeros(kblk.shape, BF16)
        vblk[...] = jnp.zeros(vblk.shape, BF16)

    _attend(qkv_ref, 0, tq, lambda: kprev_ref[...], lambda: vprev_ref[...],
            None if prev_always_valid else t > 0, sink_ref, kblk, vblk, bias_scr, o_ref)


def _swa_attention(qkv, k_prev, v_prev, sinks, batch, t_len, prev_from_qkv):
    tq = min(t_len, WINDOW)
    nt = t_len // tq
    if prev_from_qkv:
        assert tq == WINDOW
        kspec = pl.BlockSpec((WINDOW, SWA_DKV),
                             lambda b, t: (jnp.maximum(b * nt + t - 1, 0), SWA_DQ // SWA_DKV))
        vspec = pl.BlockSpec((WINDOW, SWA_DKV),
                             lambda b, t: (jnp.maximum(b * nt + t - 1, 0), SWA_DQ // SWA_DKV + 1))
        k_prev = v_prev = qkv
    else:
        kspec = pl.BlockSpec((WINDOW, SWA_DKV), lambda b, t: (b, 0))
        vspec = pl.BlockSpec((WINDOW, SWA_DKV), lambda b, t: (b, 0))
    return pl.pallas_call(
        functools.partial(_swa_kernel, tq=tq, prev_always_valid=not prev_from_qkv),
        grid=(batch, nt),
        in_specs=[
            pl.BlockSpec(memory_space=pltpu.SMEM),
            pl.BlockSpec((tq, SWA_COLS), lambda b, t: (b * nt + t, 0)),
            kspec,
            vspec,
        ],
        out_specs=pl.BlockSpec((tq, SWA_DQ), lambda b, t: (b * nt + t, 0)),
        out_shape=jax.ShapeDtypeStruct((batch * t_len, SWA_DQ), BF16),
        scratch_shapes=[
            pltpu.VMEM((SWA_KVH, SWA_GW, SWA_GROUP * SWA_NK), BF16),
            pltpu.VMEM((SWA_KVH, SWA_GROUP * SWA_NK, SWA_GW), BF16),
            pltpu.VMEM((tq, SWA_NK), F32),
        ],
        compiler_params=_params(("arbitrary", "arbitrary")),
        name="swa_attention",
    )(sinks, qkv, k_prev, v_prev)


FF_CHUNK = 1024
POST_ROW_GROUPS = 2
LAYER_FF_CHUNK = 512


def _post_rows(o, x, g_ref, wo_ref, bo_ref, wu_ref, wd_ref):
    m = o.shape[0]
    ng = POST_ROW_GROUPS if m % (POST_ROW_GROUPS * 128) == 0 else 1
    rows = [slice(i * (m // ng), (i + 1) * (m // ng)) for i in range(ng)]
    a = [_dot(o[r], wo_ref[...]) + bo_ref[...] for r in rows]
    x1 = [x[r] + _rms(a[i], g_ref[1:2, :]) for i, r in enumerate(rows)]
    h = [_rms(x1[i], g_ref[2:3, :]).astype(BF16) for i in range(ng)]
    acc = [None] * ng
    for j in range(D_FF // FF_CHUNK):
        for i in range(ng):
            u = _dot(h[i], wu_ref[:, j * FF_CHUNK:(j + 1) * FF_CHUNK])
            u = jnp.maximum(u, 0.0)
            u = (u * u).astype(BF16)
            part = _dot(u, wd_ref[j * FF_CHUNK:(j + 1) * FF_CHUNK, :])
            acc[i] = part if acc[i] is None else acc[i] + part
    y = [x1[i] + _rms(acc[i], g_ref[3:4, :]) for i in range(ng)]
    return y[0] if ng == 1 else jnp.concatenate(y, axis=0)


def _post_kernel(*refs, with_next, with_tail):
    kc_ref = vc_ref = None
    if with_tail:
        (o_ref, x_ref, g_ref, wo_ref, bo_ref, wu_ref, wd_ref, gn_ref, wn_ref, bn_ref,
         y_ref, p_ref, kc_ref, vc_ref) = refs
    elif with_next:
        o_ref, x_ref, g_ref, wo_ref, bo_ref, wu_ref, wd_ref, gn_ref, wn_ref, bn_ref, y_ref, p_ref = refs
    else:
        o_ref, x_ref, g_ref, wo_ref, bo_ref, wu_ref, wd_ref, y_ref = refs
    y = _post_rows(o_ref[...], x_ref[...], g_ref, wo_ref, bo_ref, wu_ref, wd_ref)
    y_ref[...] = y
    if with_next:
        hn = _rms(y, gn_ref[0:1, :]).astype(BF16)
        p = _dot(hn, wn_ref[...]) + bn_ref[...]
        p_ref[...] = p
        if with_tail:
            tm = p.shape[0]
            kc_ref[...] = p[tm - WINDOW:, SWA_DQ:SWA_DQ + SWA_DKV]
            vc_ref[...] = p[tm - WINDOW:, SWA_DQ + SWA_DKV:]


def _swa_layer_kernel(sink_ref, qkv_ref, kprev_ref, vprev_ref, x_ref, g_ref, wo_ref, bo_ref, wu_ref, wd_ref,
                      y_ref, kblk, vblk, bias_scr, o_scr, a_scr, *, n_tiles, tile, blocks_per_seq):
    s = pl.program_id(0)

    @pl.when(s == 0)
    def _():
        kblk[...] = jnp.zeros(kblk.shape, BF16)
        vblk[...] = jnp.zeros(vblk.shape, BF16)
        o_scr[...] = jnp.zeros(o_scr.shape, BF16)

    a_tile = jnp.minimum(s, n_tiles - 1)
    n_sub = tile // WINDOW

    pieces = []
    for j in range(n_sub):
        r0 = j * WINDOW
        if j == 0:
            k_prev, v_prev = (lambda: kprev_ref[...]), (lambda: vprev_ref[...])
            prev_valid = lax.rem(a_tile * n_sub, blocks_per_seq) != 0
        else:
            k_prev = functools.partial(lambda r: qkv_ref[r - WINDOW:r, SWA_DQ:SWA_DQ + SWA_DKV], r0)
            v_prev = functools.partial(lambda r: qkv_ref[r - WINDOW:r, SWA_DQ + SWA_DKV:], r0)
            prev_valid = None
        pieces += _attend_pieces(qkv_ref, r0, WINDOW, k_prev, v_prev, prev_valid,
                                 sink_ref, kblk, vblk, bias_scr, a_scr)

    a = _dot(o_scr[...], wo_ref[...]) + bo_ref[...]
    x1 = x_ref[...] + _rms(a, g_ref[1:2, :])
    h = _rms(x1, g_ref[2:3, :]).astype(BF16)

    n_ff = D_FF // LAYER_FF_CHUNK
    st = {"acc": None}

    def up(j):
        v = jnp.maximum(_dot(h, wu_ref[:, j * LAYER_FF_CHUNK:(j + 1) * LAYER_FF_CHUNK]), 0.0)
        st["u"] = (v * v).astype(BF16)

    def down(j):
        part = _dot(st["u"], wd_ref[j * LAYER_FF_CHUNK:(j + 1) * LAYER_FF_CHUNK, :])
        st["acc"] = part if st["acc"] is None else st["acc"] + part

    ff = []
    for j in range(n_ff):
        ff += [functools.partial(up, j), functools.partial(down, j)]
    n_pieces = len(pieces)
    for i, piece in enumerate(ff):
        while n_pieces - len(pieces) < n_pieces * (i + 1) // len(ff):
            pieces.pop(0)()
        piece()
    assert not pieces
    y_ref[...] = x1 + _rms(st["acc"], g_ref[3:4, :])
    o_scr[...] = a_scr[...]


def _swa_layer(qkv, x, sinks, g4, wo, bo, wu, wd, t_len, tile):
    n, d = x.shape
    assert t_len % tile == 0 and tile % WINDOW == 0
    n_tiles = n // tile
    per = tile // WINDOW
    a_idx = lambda s: jnp.minimum(s, n_tiles - 1)
    p_idx = lambda s: jnp.maximum(s - 1, 0)
    return pl.pallas_call(
        functools.partial(_swa_layer_kernel, n_tiles=n_tiles, tile=tile, blocks_per_seq=t_len // WINDOW),
        grid=(n_tiles + 1,),
        in_specs=[
            pl.BlockSpec(memory_space=pltpu.SMEM),
            pl.BlockSpec((tile, SWA_COLS), lambda s: (a_idx(s), 0)),
            pl.BlockSpec((WINDOW, SWA_DKV), lambda s: (jnp.maximum(a_idx(s) * per - 1, 0), SWA_DQ // SWA_DKV)),
            pl.BlockSpec((WINDOW, SWA_DKV), lambda s: (jnp.maximum(a_idx(s) * per - 1, 0), SWA_DQ // SWA_DKV + 1)),
            pl.BlockSpec((tile, d), lambda s: (p_idx(s), 0)),
            _resident((4, d)),
            _resident((d, d)),
            _resident((1, d)),
            _resident((d, D_FF)),
            _resident((D_FF, d)),
        ],
        out_specs=pl.BlockSpec((tile, d), lambda s: (p_idx(s), 0)),
        out_shape=jax.ShapeDtypeStruct((n, d), F32),
        scratch_shapes=[
            pltpu.VMEM((SWA_KVH, SWA_GW, SWA_GROUP * SWA_NK), BF16),
            pltpu.VMEM((SWA_KVH, SWA_GROUP * SWA_NK, SWA_GW), BF16),
            pltpu.VMEM((WINDOW, SWA_NK), F32),
            pltpu.VMEM((tile, SWA_DQ), BF16),
            pltpu.VMEM((tile, SWA_DQ), BF16),
        ],
        compiler_params=_params(("arbitrary",)),
        name="swa_layer",
    )(sinks, qkv, qkv, qkv, x, g4, wo, bo, wu, wd)


def _post_block(o, x, g4, wo, bo, wu, wd, tm, nxt=None, seq_tiles=None):
    n, d = x.shape
    rows = lambda i: (i, 0)
    in_specs = [
        pl.BlockSpec((tm, d), rows),
        pl.BlockSpec((tm, d), rows),
        _resident((4, d)),
        _resident((d, d)),
        _resident((1, d)),
        _resident((d, D_FF)),
        _resident((D_FF, d)),
    ]
    args = [o, x, g4, wo, bo, wu, wd]
    out_specs = [pl.BlockSpec((tm, d), rows)]
    out_shape = [jax.ShapeDtypeStruct((n, d), F32)]
    if nxt is not None:
        gn, wn, bn = nxt
        cols = wn.shape[1]
        in_specs += [_resident((4, d)), _resident((d, cols)), _resident((1, cols))]
        args += [gn, wn, bn]
        out_specs.append(pl.BlockSpec((tm, cols), rows))
        out_shape.append(jax.ShapeDtypeStruct((n, cols), F32))
    if seq_tiles is not None:
        assert nxt is not None and tm >= WINDOW and (n // tm) % seq_tiles == 0
        n_seq = n // tm // seq_tiles
        for _ in range(2):
            out_specs.append(pl.BlockSpec((WINDOW, SWA_DKV), lambda i: (i // seq_tiles, 0)))
            out_shape.append(jax.ShapeDtypeStruct((n_seq * WINDOW, SWA_DKV), F32))
    outs = pl.pallas_call(
        functools.partial(_post_kernel, with_next=nxt is not None, with_tail=seq_tiles is not None),
        grid=(n // tm,),
        in_specs=in_specs,
        out_specs=out_specs,
        out_shape=out_shape,
        compiler_params=_params(("arbitrary",)),
        name="post_block",
    )(*args)
    return outs if nxt is not None else outs[0]


def _trunk(x, gla_s0, swa_k0, swa_v0, w):
    batch, t_len, d = x.shape
    n = batch * t_len
    tm = min(ROW_TILE, n)
    assert n % tm == 0 and t_len % CHUNK == 0
    xf = x.reshape(n, d)
    og, s_fin = _gla_layer(xf, w["norm_g"][0], w["gla_w_ext"], w["gla_wb"], w["gla_bg"], w["gla_gon"],
                           gla_s0, batch, t_len)
    tail_from_kernel = swa_k0 is None and t_len % tm == 0 and tm >= WINDOW
    outs = _post_block(og, xf, w["norm_g"][0], w["gla_wo"], w["zero_bias"], w["wu"][0], w["wd"][0], tm,
                       nxt=(w["norm_g"][1], w["swa_wi"], w["swa_bi"]),
                       seq_tiles=t_len // tm if tail_from_kernel else None)
    x1, qkv = outs[0], outs[1]
    if swa_k0 is None and t_len % SWA_LAYER_TILE == 0:
        y = _swa_layer(qkv, x1, w["sinks"], w["norm_g"][1], w["swa_wo"], w["swa_bo"], w["wu"][1], w["wd"][1],
                       t_len, SWA_LAYER_TILE)
    else:
        if swa_k0 is None:
            oa = _swa_attention(qkv, None, None, w["sinks"], batch, t_len, True)
        else:
            kp = swa_k0.reshape(batch * WINDOW, SWA_DKV)
            vp = swa_v0.reshape(batch * WINDOW, SWA_DKV)
            oa = _swa_attention(qkv, kp, vp, w["sinks"], batch, t_len, False)
        y = _post_block(oa, x1, w["norm_g"][1], w["swa_wo"], w["swa_bo"], w["wu"][1], w["wd"][1], tm)
    if tail_from_kernel:
        k_cache = outs[2].reshape(batch, WINDOW, SWA_KVH, SWA_HD)
        v_cache = outs[3].reshape(batch, WINDOW, SWA_KVH, SWA_HD)
    else:
        keep = min(WINDOW, t_len) if swa_k0 is None else swa_k0.shape[1]
        new = min(keep, t_len)
        tail = qkv.reshape(batch, t_len, SWA_COLS)[:, t_len - new:, :]
        k_new = tail[:, :, SWA_DQ:SWA_DQ + SWA_DKV].reshape(batch, new, SWA_KVH, SWA_HD)
        v_new = tail[:, :, SWA_DQ + SWA_DKV:].reshape(batch, new, SWA_KVH, SWA_HD)
        if swa_k0 is None:
            k_cache, v_cache = k_new, v_new
        else:
            k_cache = jnp.concatenate([swa_k0[:, new:], k_new], axis=1)
            v_cache = jnp.concatenate([swa_v0[:, new:], v_new], axis=1)
    return y.reshape(batch, t_len, d), s_fin[None], k_cache[None], v_cache[None]


def kernel(x_prompt, x_sample, state_gla, cache_swa_k, cache_swa_v, gla_w_in, gla_w_gate_a, gla_w_gate_b, gla_b_gate, gla_g_onorm, gla_w_out, swa_w_in, swa_b_in, swa_sinks, swa_w_out, swa_b_out, norm_g, mlp_w_up, mlp_w_down):
    d = D_MODEL
    pad_a = jnp.zeros((d, GLA_GATE_PAD - GLA_GATE_RANK), F32)
    pad_b = jnp.zeros((GLA_GATE_PAD - GLA_GATE_RANK, GLA_DQK), F32)
    w = {
        "norm_g": norm_g,
        "gla_w_ext": jnp.concatenate([gla_w_in[0], gla_w_gate_a[0], pad_a], axis=1).astype(BF16),
        "gla_wb": jnp.concatenate([gla_w_gate_b[0], pad_b], axis=0).astype(BF16),
        "gla_bg": gla_b_gate[0].reshape(1, GLA_DQK),
        "gla_gon": gla_g_onorm[0].reshape(1, GLA_DVT),
        "gla_wo": gla_w_out[0].astype(BF16),
        "zero_bias": jnp.zeros((1, d), F32),
        "swa_wi": swa_w_in[0].astype(BF16),
        "swa_bi": swa_b_in[0].reshape(1, SWA_COLS),
        "sinks": swa_sinks[0],
        "swa_wo": swa_w_out[0].astype(BF16),
        "swa_bo": swa_b_out[0].reshape(1, d),
        "wu": mlp_w_up.astype(BF16),
        "wd": mlp_w_down.astype(BF16),
    }
    y_p, s_p, k_p, v_p = _trunk(x_prompt, None, None, None, w)
    y_s, s_s, k_s, v_s = _trunk(x_sample, state_gla[0], cache_swa_k[0], cache_swa_v[0], w)
    return (y_p, y_s, s_p, s_s, k_p, v_p, k_s, v_s)
```

```python
import functools
import math

import numpy as np
import jax
import jax.numpy as jnp
from jax import lax
from jax.experimental import pallas as pl
from jax.experimental.pallas import tpu as pltpu

F32 = jnp.float32
BF16 = jnp.bfloat16

D_MODEL = 1024
D_FF = 4 * D_MODEL
EPS = 1e-6
CHUNK = 64
LOG2E = math.log2(math.e)
GLA_HEADS = 4
GLA_DK = 128
GLA_DV = 256
GLA_DQK = GLA_HEADS * GLA_DK
GLA_DVT = GLA_HEADS * GLA_DV
GLA_GATE_RANK = 16
GLA_GATE_NORM = 16.0
GLA_GATE_PAD = 128
GLA_PROJ_COLS = 2 * GLA_DQK + 2 * GLA_DVT + GLA_GATE_PAD
GLA_MAX_CHUNK = 256
GLA_PROJ_TILE = 256
GLA_CHUNKS_PER_STEP = 4
GLA_SAFE_DECAY = 60.0
SWA_HD = 64
SWA_QH = 16
SWA_KVH = 4
SWA_GROUP = 4
SWA_DQ = SWA_QH * SWA_HD
SWA_DKV = SWA_KVH * SWA_HD
SWA_COLS = SWA_DQ + 2 * SWA_DKV
SWA_GW = SWA_GROUP * SWA_HD
WINDOW = 128
SWA_NK = 2 * WINDOW
ROW_TILE = 512
SWA_LAYER_TILE = 512
VMEM_LIMIT_BYTES = 60 * 1024 * 1024


def _dot(a, b):
    return jnp.dot(a, b, preferred_element_type=F32)


def _dot_nt(a, b):
    return lax.dot_general(a, b, (((1,), (1,)), ((), ())), preferred_element_type=F32)


def _dot_tn(a, b):
    return lax.dot_general(a, b, (((0,), (0,)), ((), ())), preferred_element_type=F32)


def _rms(x, g):
    return x * lax.rsqrt(jnp.mean(x * x, axis=-1, keepdims=True) + EPS) * g


def _split_bf16(x):
    hi = x.astype(BF16)
    lo = (x - hi.astype(F32)).astype(BF16)
    return hi, lo


def _params(semantics):
    return pltpu.CompilerParams(dimension_semantics=semantics, vmem_limit_bytes=VMEM_LIMIT_BYTES)


def _resident(shape):
    zeros = (0,) * len(shape)
    return pl.BlockSpec(shape, lambda *_: zeros, pipeline_mode=pl.Buffered(1))


def _gla_level_matrices(c):
    levels = []
    s = c
    while s >= 2:
        levels.append(s)
        s //= 2
    t = np.arange(c)[None, :]
    r = np.arange(c)[:, None]
    mats = []
    for s in levels:
        mid = (r // s) * s + s // 2
        mq = (r >= mid) & (t >= mid) & (t <= r)
        mk = (r < mid) & (t > r) & (t <= mid - 1)
        mats.append(mq)
        mats.append(mk)
    return np.stack(mats).astype(np.float32), levels


def _gla_kernel(*refs, c, levels, has_s0, nt, cps):
    if has_s0:
        (x_ref, g_ref, w_ref, wb_ref, bg_ref, gon_ref, tri_ref, lev_ref, s0_ref,
         og_ref, s_ref, a_scr, sprev, proj_a, proj_b) = refs
    else:
        (x_ref, g_ref, w_ref, wb_ref, bg_ref, gon_ref, tri_ref, lev_ref,
         og_ref, s_ref, a_scr, sprev, proj_a, proj_b) = refs
        s0_ref = None
    s = pl.program_id(0)
    t0 = jnp.maximum(s - 1, 0) * cps

    @pl.when(s == 0)
    def _():
        proj_a[...] = jnp.zeros(proj_a.shape, F32)
        proj_b[...] = jnp.zeros(proj_b.shape, F32)

    scale = GLA_DK ** -0.5
    row = lax.broadcasted_iota(jnp.int32, (c, c), 0)
    col = lax.broadcasted_iota(jnp.int32, (c, c), 1)
    nb = min(c, GLA_DK)

    def projection_pieces(dst):
        h = _rms(x_ref[...], g_ref[0:1, :]).astype(BF16)

        def tile(c0, c1):
            dst[:, c0:c1] = _dot(h, w_ref[:, c0:c1])

        edges = list(range(0, GLA_PROJ_COLS, GLA_PROJ_TILE)) + [GLA_PROJ_COLS]
        return [functools.partial(tile, c0, c1) for c0, c1 in zip(edges[:-1], edges[1:])]

    def step(proj_ref, j, robust, fillers=()):
        rows = slice(j * c, (j + 1) * c)
        proj = proj_ref.at[rows, :]
        first = lax.rem(t0 + j, nt) == 0
        fillers = list(fillers)
        n_slots = 13
        per_slot = -(-len(fillers) // n_slots)

        def fill():
            for _ in range(per_slot):
                if fillers:
                    fillers.pop(0)()

        qk = proj[:, :2 * GLA_DQK]
        v_all = proj[:, 2 * GLA_DQK:2 * GLA_DQK + GLA_DVT].astype(BF16)
        r_all = proj[:, 2 * GLA_DQK + GLA_DVT:2 * GLA_DQK + 2 * GLA_DVT]
        ga = proj[:, 2 * GLA_DQK + 2 * GLA_DVT:].astype(BF16)
        gp = _dot(ga, wb_ref[...]) + bg_ref[...]
        fill()
        soft = jnp.log2(1.0 + jnp.exp2(jnp.abs(gp) * (-LOG2E)))
        lf = jnp.minimum(gp, 0.0) * (LOG2E / GLA_GATE_NORM) - soft * (1.0 / GLA_GATE_NORM)
        lf_hi, lf_lo = _split_bf16(lf)
        fill()
        tri = tri_ref[...]
        b = _dot(tri, lf_hi) + _dot(tri, lf_lo)
        b_last = b[c - 1:c, :]
        fill()

        if robust:
            for hd in range(GLA_HEADS):
                cs = slice(hd * GLA_DK, (hd + 1) * GLA_DK)
                ks = slice(GLA_DQK + hd * GLA_DK, GLA_DQK + (hd + 1) * GLA_DK)
                a_scr[hd] = jnp.where(row == col,
                                      _dot_nt((qk[:, cs] * scale).astype(BF16), qk[:, ks].astype(BF16)), 0.0)
            for li, s in enumerate(levels):
                mq = lev_ref[2 * li]
                mk = lev_ref[2 * li + 1]
                dq = _dot(mq, lf_hi) + _dot(mq, lf_lo)
                dk = _dot(mk, lf_hi) + _dot(mk, lf_lo)
                half = s // 2
                sh = s.bit_length() - 1
                same = lax.shift_right_logical(row, sh) == lax.shift_right_logical(col, sh)
                mask = same & ((row & (s - 1)) >= half) & ((col & (s - 1)) < half)
                for hd in range(GLA_HEADS):
                    cs = slice(hd * GLA_DK, (hd + 1) * GLA_DK)
                    ks = slice(GLA_DQK + hd * GLA_DK, GLA_DQK + (hd + 1) * GLA_DK)
                    qq = (qk[:, cs] * scale * jnp.exp2(dq[:, cs])).astype(BF16)
                    kk = (qk[:, ks] * jnp.exp2(dk[:, cs])).astype(BF16)
                    a_scr[hd] = a_scr[hd] + jnp.where(mask, _dot_nt(qq, kk), 0.0)

        heads = range(GLA_HEADS)
        cs = [slice(hd * GLA_DK, (hd + 1) * GLA_DK) for hd in heads]
        ks = [slice(GLA_DQK + hd * GLA_DK, GLA_DQK + (hd + 1) * GLA_DK) for hd in heads]
        vs = [slice(hd * GLA_DV, (hd + 1) * GLA_DV) for hd in heads]
        qe = [(qk[:, cs[hd]] * scale * jnp.exp2(b[:, cs[hd]])).astype(BF16) for hd in heads]
        fill()
        if robust:
            s_old = [sprev[j, hd] for hd in heads]
            a = [a_scr[hd].astype(BF16) for hd in heads]
        else:
            init = [s0_ref[0, hd] if has_s0 else jnp.zeros((GLA_DK, GLA_DV), F32) for hd in heads]
            s_old = [jnp.where(first, init[hd], s_ref[0, hd]) for hd in heads]
            for hd in heads:
                sprev[j, hd] = s_old[hd]
            ke = [(qk[:, ks[hd]] * jnp.exp2(-b[:, cs[hd]])).astype(BF16) for hd in heads]
            fill()
            a = [jnp.where(row >= col, _dot_nt(qe[hd], ke[hd]), 0.0).astype(BF16) for hd in heads]
        fill()
        o = [_dot(qe[hd], s_old[hd].astype(BF16)) + _dot(a[hd], v_all[:, vs[hd]]) for hd in heads]
        fill()
        if not robust:
            kd = [(qk[:, ks[hd]] * jnp.exp2(b_last[:, cs[hd]] - b[:, cs[hd]])).astype(BF16) for hd in heads]
            fill()
            fill()
            for hd in heads:
                dec = jnp.exp2(jnp.transpose(b[c - nb:c, cs[hd]])[:, nb - 1:nb])
                s_ref[0, hd] = s_old[hd] * dec + _dot_tn(kd[hd], v_all[:, vs[hd]])
        for hd in heads:
            fill()
            on = o[hd] * lax.rsqrt(jnp.mean(o[hd] * o[hd], axis=-1, keepdims=True) + EPS) * gon_ref[:, vs[hd]]
            hr = 0.5 * r_all[:, vs[hd]]
            og_ref[rows, vs[hd]] = (on * (hr + hr * jnp.tanh(hr))).astype(BF16)
        while fillers:
            fillers.pop(0)()
        return jnp.min(b)

    def body(src, dst):
        pieces = projection_pieces(dst)
        share = -(-len(pieces) // cps)
        b_min = [step(src, j, False, pieces[j * share:(j + 1) * share]) for j in range(cps)]
        for j in range(cps):
            @pl.when(b_min[j] < -GLA_SAFE_DECAY * LOG2E)
            def _(j=j):
                step(src, j, True)

    @pl.when(lax.rem(s, 2) == 0)
    def _():
        body(proj_b, proj_a)

    @pl.when(lax.rem(s, 2) == 1)
    def _():
        body(proj_a, proj_b)


def _gla_layer(x, g4, w_ext, wb, bg, gon, s0, batch, t_len):
    d = x.shape[1]
    c = min(t_len, GLA_MAX_CHUNK)
    nt = t_len // c
    cps = GLA_CHUNKS_PER_STEP if nt % GLA_CHUNKS_PER_STEP == 0 else 1
    n_steps = batch * nt // cps
    lev_np, levels = _gla_level_matrices(c)
    lev = jnp.asarray(lev_np, BF16)
    tri = jnp.asarray(np.tril(np.ones((c, c), np.float32)), BF16)
    has_s0 = s0 is not None
    p_idx = lambda s: jnp.minimum(s, n_steps - 1)
    t_idx = lambda s: jnp.maximum(s - 1, 0)
    in_specs = [
        pl.BlockSpec((cps * c, d), lambda s: (p_idx(s), 0)),
        _resident((4, d)),
        _resident((d, GLA_PROJ_COLS)),
        _resident((GLA_GATE_PAD, GLA_DQK)),
        _resident((1, GLA_DQK)),
        _resident((1, GLA_DVT)),
        _resident((c, c)),
        _resident((2 * len(levels), c, c)),
    ]
    args = [x, g4, w_ext, wb, bg, gon, tri, lev]
    state_spec = pl.BlockSpec((1, GLA_HEADS, GLA_DK, GLA_DV), lambda s: (t_idx(s) * cps // nt, 0, 0, 0))
    if has_s0:
        in_specs.append(state_spec)
        args.append(s0)
    og, s_fin = pl.pallas_call(
        functools.partial(_gla_kernel, c=c, levels=tuple(levels), has_s0=has_s0, nt=nt, cps=cps),
        grid=(n_steps + 1,),
        in_specs=in_specs,
        out_specs=[
            pl.BlockSpec((cps * c, GLA_DVT), lambda s: (t_idx(s), 0)),
            state_spec,
        ],
        out_shape=[
            jax.ShapeDtypeStruct((batch * t_len, GLA_DVT), BF16),
            jax.ShapeDtypeStruct((batch, GLA_HEADS, GLA_DK, GLA_DV), F32),
        ],
        scratch_shapes=[
            pltpu.VMEM((GLA_HEADS, c, c), F32),
            pltpu.VMEM((cps, GLA_HEADS, GLA_DK, GLA_DV), F32),
            pltpu.VMEM((cps * c, GLA_PROJ_COLS), F32),
            pltpu.VMEM((cps * c, GLA_PROJ_COLS), F32),
        ],
        compiler_params=_params(("arbitrary",)),
        name="gla_layer",
    )(*args)
    return og, s_fin


def _attend(qkv_ref, r0, tq, k_prev, v_prev, prev_valid, sink_ref, kblk, vblk, bias_scr, o_ref):
    for piece in _attend_pieces(qkv_ref, r0, tq, k_prev, v_prev, prev_valid, sink_ref, kblk, vblk, bias_scr, o_ref):
        piece()


def _attend_pieces(qkv_ref, r0, tq, k_prev, v_prev, prev_valid, sink_ref, kblk, vblk, bias_scr, o_ref):
    def prologue():
        pad = SWA_NK - WINDOW - tq
        k_parts = [k_prev(), qkv_ref[r0:r0 + tq, SWA_DQ:SWA_DQ + SWA_DKV]]
        v_parts = [v_prev(), qkv_ref[r0:r0 + tq, SWA_DQ + SWA_DKV:]]
        if pad:
            k_parts.append(jnp.zeros((pad, SWA_DKV), F32))
            v_parts.append(jnp.zeros((pad, SWA_DKV), F32))
        kt = jnp.concatenate(k_parts, axis=0).T.astype(BF16)
        vc = jnp.concatenate(v_parts, axis=0).astype(BF16)
        for kh in range(SWA_KVH):
            for g in range(SWA_GROUP):
                kblk[kh, g * SWA_HD:(g + 1) * SWA_HD, g * SWA_NK:(g + 1) * SWA_NK] = kt[kh * SWA_HD:(kh + 1) * SWA_HD, :]
                vblk[kh, g * SWA_NK:(g + 1) * SWA_NK, g * SWA_HD:(g + 1) * SWA_HD] = vc[:, kh * SWA_HD:(kh + 1) * SWA_HD]
        qrow = lax.broadcasted_iota(jnp.int32, (tq, SWA_NK), 0)
        kcol = lax.broadcasted_iota(jnp.int32, (tq, SWA_NK), 1)
        lo = lax.shift_right_logical(qrow, 6) * CHUNK
        mask = (kcol >= lo) & (kcol < lo + WINDOW + CHUNK) & (kcol < WINDOW + tq)
        if prev_valid is not None:
            mask = mask & jnp.logical_or(kcol >= WINDOW, prev_valid)
        bias_scr[...] = jnp.where(mask, 0.0, -jnp.inf)

    state = {}

    def head(kh, g):
        if g == 0:
            state["q4"] = (qkv_ref[r0:r0 + tq, kh * SWA_GW:(kh + 1) * SWA_GW]
                           * (SWA_HD ** -0.5 * LOG2E)).astype(BF16)
            state["acc"] = None
            state["linv"] = []
        sg = _dot(state["q4"], kblk[kh, :, g * SWA_NK:(g + 1) * SWA_NK]) + bias_scr[...]
        sink = sink_ref[kh * SWA_GROUP + g] * LOG2E
        m = jnp.maximum(jnp.max(sg, axis=-1, keepdims=True), sink)
        p = jnp.exp2(sg - m)
        state["linv"].append(1.0 / (jnp.sum(p, axis=-1, keepdims=True) + jnp.exp2(sink - m)))
        part = _dot(p.astype(BF16), vblk[kh, g * SWA_NK:(g + 1) * SWA_NK, :])
        state["acc"] = part if state["acc"] is None else state["acc"] + part
        if g == SWA_GROUP - 1:
            linv = state["linv"]
            lane = lax.broadcasted_iota(jnp.int32, (tq, SWA_GW), 1)
            norm = jnp.where(lane < SWA_HD, linv[0],
                             jnp.where(lane < 2 * SWA_HD, linv[1],
                                       jnp.where(lane < 3 * SWA_HD, linv[2], linv[3])))
            o_ref[r0:r0 + tq, kh * SWA_GW:(kh + 1) * SWA_GW] = (state["acc"] * norm).astype(BF16)

    return [prologue] + [functools.partial(head, kh, g) for kh in range(SWA_KVH) for g in range(SWA_GROUP)]


def _swa_kernel(sink_ref, qkv_ref, kprev_ref, vprev_ref, o_ref, kblk, vblk, bias_scr, *, tq, prev_always_valid):
    t = pl.program_id(1)

    @pl.when(jnp.logical_and(pl.program_id(0) == 0, t == 0))
    def _():
        kblk[...] = jnp.zeros(kblk.shape, BF16)
        vblk[...] = jnp.zeros(vblk.shape, BF16)

    _attend(qkv_ref, 0, tq, lambda: kprev_ref[...], lambda: vprev_ref[...],
            None if prev_always_valid else t > 0, sink_ref, kblk, vblk, bias_scr, o_ref)


def _swa_attention(qkv, k_prev, v_prev, sinks, batch, t_len, prev_from_qkv):
    tq = min(t_len, WINDOW)
    nt = t_len // tq
    if prev_from_qkv:
        assert tq == WINDOW
        kspec = pl.BlockSpec((WINDOW, SWA_DKV),
                             lambda b, t: (jnp.maximum(b * nt + t - 1, 0), SWA_DQ // SWA_DKV))
        vspec = pl.BlockSpec((WINDOW, SWA_DKV),
                             lambda b, t: (jnp.maximum(b * nt + t - 1, 0), SWA_DQ // SWA_DKV + 1))
        k_prev = v_prev = qkv
    else:
        kspec = pl.BlockSpec((WINDOW, SWA_DKV), lambda b, t: (b, 0))
        vspec = pl.BlockSpec((WINDOW, SWA_DKV), lambda b, t: (b, 0))
    return pl.pallas_call(
        functools.partial(_swa_kernel, tq=tq, prev_always_valid=not prev_from_qkv),
        grid=(batch, nt),
        in_specs=[
            pl.BlockSpec(memory_space=pltpu.SMEM),
            pl.BlockSpec((tq, SWA_COLS), lambda b, t: (b * nt + t, 0)),
            kspec,
            vspec,
        ],
        out_specs=pl.BlockSpec((tq, SWA_DQ), lambda b, t: (b * nt + t, 0)),
        out_shape=jax.ShapeDtypeStruct((batch * t_len, SWA_DQ), BF16),
        scratch_shapes=[
            pltpu.VMEM((SWA_KVH, SWA_GW, SWA_GROUP * SWA_NK), BF16),
            pltpu.VMEM((SWA_KVH, SWA_GROUP * SWA_NK, SWA_GW), BF16),
            pltpu.VMEM((tq, SWA_NK), F32),
        ],
        compiler_params=_params(("arbitrary", "arbitrary")),
        name="swa_attention",
    )(sinks, qkv, k_prev, v_prev)


FF_CHUNK = 1024
POST_ROW_GROUPS = 2
LAYER_FF_CHUNK = 512


def _post_rows(o, x, g_ref, wo_ref, bo_ref, wu_ref, wd_ref):
    m = o.shape[0]
    ng = POST_ROW_GROUPS if m % (POST_ROW_GROUPS * 128) == 0 else 1
    rows = [slice(i * (m // ng), (i + 1) * (m // ng)) for i in range(ng)]
    a = [_dot(o[r], wo_ref[...]) + bo_ref[...] for r in rows]
    x1 = [x[r] + _rms(a[i], g_ref[1:2, :]) for i, r in enumerate(rows)]
    h = [_rms(x1[i], g_ref[2:3, :]).astype(BF16) for i in range(ng)]
    acc = [None] * ng
    for j in range(D_FF // FF_CHUNK):
        for i in range(ng):
            u = _dot(h[i], wu_ref[:, j * FF_CHUNK:(j + 1) * FF_CHUNK])
            u = jnp.maximum(u, 0.0)
            u = (u * u).astype(BF16)
            part = _dot(u, wd_ref[j * FF_CHUNK:(j + 1) * FF_CHUNK, :])
            acc[i] = part if acc[i] is None else acc[i] + part
    y = [x1[i] + _rms(acc[i], g_ref[3:4, :]) for i in range(ng)]
    return y[0] if ng == 1 else jnp.concatenate(y, axis=0)


def _post_kernel(*refs, with_next, with_tail):
    kc_ref = vc_ref = None
    if with_tail:
        (o_ref, x_ref, g_ref, wo_ref, bo_ref, wu_ref, wd_ref, gn_ref, wn_ref, bn_ref,
         y_ref, p_ref, kc_ref, vc_ref) = refs
    elif with_next:
        o_ref, x_ref, g_ref, wo_ref, bo_ref, wu_ref, wd_ref, gn_ref, wn_ref, bn_ref, y_ref, p_ref = refs
    else:
        o_ref, x_ref, g_ref, wo_ref, bo_ref, wu_ref, wd_ref, y_ref = refs
    y = _post_rows(o_ref[...], x_ref[...], g_ref, wo_ref, bo_ref, wu_ref, wd_ref)
    y_ref[...] = y
    if with_next:
        hn = _rms(y, gn_ref[0:1, :]).astype(BF16)
        p = _dot(hn, wn_ref[...]) + bn_ref[...]
        p_ref[...] = p
        if with_tail:
            tm = p.shape[0]
            kc_ref[...] = p[tm - WINDOW:, SWA_DQ:SWA_DQ + SWA_DKV]
            vc_ref[...] = p[tm - WINDOW:, SWA_DQ + SWA_DKV:]


def _swa_layer_kernel(sink_ref, qkv_ref, kprev_ref, vprev_ref, x_ref, g_ref, wo_ref, bo_ref, wu_ref, wd_ref,
                      y_ref, kblk, vblk, bias_scr, o_scr, a_scr, *, n_tiles, tile, blocks_per_seq):
    s = pl.program_id(0)

    @pl.when(s == 0)
    def _():
        kblk[...] = jnp.zeros(kblk.shape, BF16)
        vblk[...] = jnp.zeros(vblk.shape, BF16)
        o_scr[...] = jnp.zeros(o_scr.shape, BF16)

    a_tile = jnp.minimum(s, n_tiles - 1)
    n_sub = tile // WINDOW

    pieces = []
    for j in range(n_sub):
        r0 = j * WINDOW
        if j == 0:
            k_prev, v_prev = (lambda: kprev_ref[...]), (lambda: vprev_ref[...])
            prev_valid = lax.rem(a_tile * n_sub, blocks_per_seq) != 0
        else:
            k_prev = functools.partial(lambda r: qkv_ref[r - WINDOW:r, SWA_DQ:SWA_DQ + SWA_DKV], r0)
            v_prev = functools.partial(lambda r: qkv_ref[r - WINDOW:r, SWA_DQ + SWA_DKV:], r0)
            prev_valid = None
        pieces += _attend_pieces(qkv_ref, r0, WINDOW, k_prev, v_prev, prev_valid,
                                 sink_ref, kblk, vblk, bias_scr, a_scr)

    a = _dot(o_scr[...], wo_ref[...]) + bo_ref[...]
    x1 = x_ref[...] + _rms(a, g_ref[1:2, :])
    h = _rms(x1, g_ref[2:3, :]).astype(BF16)

    n_ff = D_FF // LAYER_FF_CHUNK
    st = {"acc": None}

    def up(j):
        v = jnp.maximum(_dot(h, wu_ref[:, j * LAYER_FF_CHUNK:(j + 1) * LAYER_FF_CHUNK]), 0.0)
        st["u"] = (v * v).astype(BF16)

    def down(j):
        part = _dot(st["u"], wd_ref[j * LAYER_FF_CHUNK:(j + 1) * LAYER_FF_CHUNK, :])
        st["acc"] = part if st["acc"] is None else st["acc"] + part

    ff = []
    for j in range(n_ff):
        ff += [functools.partial(up, j), functools.partial(down, j)]
    n_pieces = len(pieces)
    for i, piece in enumerate(ff):
        while n_pieces - len(pieces) < n_pieces * (i + 1) // len(ff):
            pieces.pop(0)()
        piece()
    assert not pieces
    y_ref[...] = x1 + _rms(st["acc"], g_ref[3:4, :])
    o_scr[...] = a_scr[...]


def _swa_layer(qkv, x, sinks, g4, wo, bo, wu, wd, t_len, tile):
    n, d = x.shape
    assert t_len % tile == 0 and tile % WINDOW == 0
    n_tiles = n // tile
    per = tile // WINDOW
    a_idx = lambda s: jnp.minimum(s, n_tiles - 1)
    p_idx = lambda s: jnp.maximum(s - 1, 0)
    return pl.pallas_call(
        functools.partial(_swa_layer_kernel, n_tiles=n_tiles, tile=tile, blocks_per_seq=t_len // WINDOW),
        grid=(n_tiles + 1,),
        in_specs=[
            pl.BlockSpec(memory_space=pltpu.SMEM),
            pl.BlockSpec((tile, SWA_COLS), lambda s: (a_idx(s), 0)),
            pl.BlockSpec((WINDOW, SWA_DKV), lambda s: (jnp.maximum(a_idx(s) * per - 1, 0), SWA_DQ // SWA_DKV)),
            pl.BlockSpec((WINDOW, SWA_DKV), lambda s: (jnp.maximum(a_idx(s) * per - 1, 0), SWA_DQ // SWA_DKV + 1)),
            pl.BlockSpec((tile, d), lambda s: (p_idx(s), 0)),
            _resident((4, d)),
            _resident((d, d)),
            _resident((1, d)),
            _resident((d, D_FF)),
            _resident((D_FF, d)),
        ],
        out_specs=pl.BlockSpec((tile, d), lambda s: (p_idx(s), 0)),
        out_shape=jax.ShapeDtypeStruct((n, d), F32),
        scratch_shapes=[
            pltpu.VMEM((SWA_KVH, SWA_GW, SWA_GROUP * SWA_NK), BF16),
            pltpu.VMEM((SWA_KVH, SWA_GROUP * SWA_NK, SWA_GW), BF16),
            pltpu.VMEM((WINDOW, SWA_NK), F32),
            pltpu.VMEM((tile, SWA_DQ), BF16),
            pltpu.VMEM((tile, SWA_DQ), BF16),
        ],
        compiler_params=_params(("arbitrary",)),
        name="swa_layer",
    )(sinks, qkv, qkv, qkv, x, g4, wo, bo, wu, wd)


def _post_block(o, x, g4, wo, bo, wu, wd, tm, nxt=None, seq_tiles=None):
    n, d = x.shape
    rows = lambda i: (i, 0)
    in_specs = [
        pl.BlockSpec((tm, d), rows),
        pl.BlockSpec((tm, d), rows),
        _resident((4, d)),
        _resident((d, d)),
        _resident((1, d)),
        _resident((d, D_FF)),
        _resident((D_FF, d)),
    ]
    args = [o, x, g4, wo, bo, wu, wd]
    out_specs = [pl.BlockSpec((tm, d), rows)]
    out_shape = [jax.ShapeDtypeStruct((n, d), F32)]
    if nxt is not None:
        gn, wn, bn = nxt
        cols = wn.shape[1]
        in_specs += [_resident((4, d)), _resident((d, cols)), _resident((1, cols))]
        args += [gn, wn, bn]
        out_specs.append(pl.BlockSpec((tm, cols), rows))
        out_shape.append(jax.ShapeDtypeStruct((n, cols), F32))
    if seq_tiles is not None:
        assert nxt is not None and tm >= WINDOW and (n // tm) % seq_tiles == 0
        n_seq = n // tm // seq_tiles
        for _ in range(2):
            out_specs.append(pl.BlockSpec((WINDOW, SWA_DKV), lambda i: (i // seq_tiles, 0)))
            out_shape.append(jax.ShapeDtypeStruct((n_seq * WINDOW, SWA_DKV), F32))
    outs = pl.pallas_call(
        functools.partial(_post_kernel, with_next=nxt is not None, with_tail=seq_tiles is not None),
        grid=(n // tm,),
        in_specs=in_specs,
        out_specs=out_specs,
        out_shape=out_shape,
        compiler_params=_params(("arbitrary",)),
        name="post_block",
    )(*args)
    return outs if nxt is not None else outs[0]


def _trunk(x, gla_s0, swa_k0, swa_v0, w):
    batch, t_len, d = x.shape
    n = batch * t_len
    tm = min(ROW_TILE, n)
    assert n % tm == 0 and t_len % CHUNK == 0
    xf = x.reshape(n, d)
    og, s_fin = _gla_layer(xf, w["norm_g"][0], w["gla_w_ext"], w["gla_wb"], w["gla_bg"], w["gla_gon"],
                           gla_s0, batch, t_len)
    tail_from_kernel = swa_k0 is None and t_len % tm == 0 and tm >= WINDOW
    outs = _post_block(og, xf, w["norm_g"][0], w["gla_wo"], w["zero_bias"], w["wu"][0], w["wd"][0], tm,
                       nxt=(w["norm_g"][1], w["swa_wi"], w["swa_bi"]),
                       seq_tiles=t_len // tm if tail_from_kernel else None)
    x1, qkv = outs[0], outs[1]
    if swa_k0 is None and t_len % SWA_LAYER_TILE == 0:
        y = _swa_layer(qkv, x1, w["sinks"], w["norm_g"][1], w["swa_wo"], w["swa_bo"], w["wu"][1], w["wd"][1],
                       t_len, SWA_LAYER_TILE)
    else:
        if swa_k0 is None:
            oa = _swa_attention(qkv, None, None, w["sinks"], batch, t_len, True)
        else:
            kp = swa_k0.reshape(batch * WINDOW, SWA_DKV)
            vp = swa_v0.reshape(batch * WINDOW, SWA_DKV)
            oa = _swa_attention(qkv, kp, vp, w["sinks"], batch, t_len, False)
        y = _post_block(oa, x1, w["norm_g"][1], w["swa_wo"], w["swa_bo"], w["wu"][1], w["wd"][1], tm)
    if tail_from_kernel:
        k_cache = outs[2].reshape(batch, WINDOW, SWA_KVH, SWA_HD)
        v_cache = outs[3].reshape(batch, WINDOW, SWA_KVH, SWA_HD)
    else:
        keep = min(WINDOW, t_len) if swa_k0 is None else swa_k0.shape[1]
        new = min(keep, t_len)
        tail = qkv.reshape(batch, t_len, SWA_COLS)[:, t_len - new:, :]
        k_new = tail[:, :, SWA_DQ:SWA_DQ + SWA_DKV].reshape(batch, new, SWA_KVH, SWA_HD)
        v_new = tail[:, :, SWA_DQ + SWA_DKV:].reshape(batch, new, SWA_KVH, SWA_HD)
        if swa_k0 is None:
            k_cache, v_cache = k_new, v_new
        else:
            k_cache = jnp.concatenate([swa_k0[:, new:], k_new], axis=1)
            v_cache = jnp.concatenate([swa_v0[:, new:], v_new], axis=1)
    return y.reshape(batch, t_len, d), s_fin[None], k_cache[None], v_cache[None]


def kernel(x_prompt, x_sample, state_gla, cache_swa_k, cache_swa_v, gla_w_in, gla_w_gate_a, gla_w_gate_b, gla_b_gate, gla_g_onorm, gla_w_out, swa_w_in, swa_b_in, swa_sinks, swa_w_out, swa_b_out, norm_g, mlp_w_up, mlp_w_down):
    d = D_MODEL
    pad_a = jnp.zeros((d, GLA_GATE_PAD - GLA_GATE_RANK), F32)
    pad_b = jnp.zeros((GLA_GATE_PAD - GLA_GATE_RANK, GLA_DQK), F32)
    w = {
        "norm_g": norm_g,
        "gla_w_ext": jnp.concatenate([gla_w_in[0], gla_w_gate_a[0], pad_a], axis=1).astype(BF16),
        "gla_wb": jnp.concatenate([gla_w_gate_b[0], pad_b], axis=0).astype(BF16),
        "gla_bg": gla_b_gate[0].reshape(1, GLA_DQK),
        "gla_gon": gla_g_onorm[0].reshape(1, GLA_DVT),
        "gla_wo": gla_w_out[0].astype(BF16),
        "zero_bias": jnp.zeros((1, d), F32),
        "swa_wi": swa_w_in[0].astype(BF16),
        "swa_bi": swa_b_in[0].reshape(1, SWA_COLS),
        "sinks": swa_sinks[0],
        "swa_wo": swa_w_out[0].astype(BF16),
        "swa_bo": swa_b_out[0].reshape(1, d),
        "wu": mlp_w_up.astype(BF16),
        "wd": mlp_w_down.astype(BF16),
    }
    y_p, s_p, k_p, v_p = _trunk(x_prompt, None, None, None, w)
    y_s, s_s, k_s, v_s = _trunk(x_sample, state_gla[0], cache_swa_k[0], cache_swa_v[0], w)
    return (y_p, y_s, s_p, s_s, k_p, v_p, k_s, v_s)
```
